```python
import jax, jax.numpy as jnp
from jax import lax
import numpy as np

D_MODEL = 2048
BATCH = 4
SEQ = 4096
DEPTH = 1

CTX_LEN = 256
GRID_W = 64
D_MIX = D_MODEL
D_MLSTM = D_MIX // 2
D_LRU = D_MIX - D_MLSTM
N_MLSTM_HEADS = 4
MLSTM_HEAD_DIM = D_MLSTM // N_MLSTM_HEADS
MLSTM_CHUNK = 128
N_GATE_COLS = 4 * N_MLSTM_HEADS
N_LRU_BLOCKS = 8
LRU_BLOCK = D_LRU // N_LRU_BLOCKS
LRU_C = 8.0
CONV_W = 4
CONV_PAD = (2, 1)
D_FF = ((8 * D_MODEL // 3 + 255) // 256) * 256
P_IN = 4 * D_MLSTM + N_GATE_COLS + 2 * D_LRU
EPS = 1e-6

kernel_name = 'hymba_mlstm_rglru_dit_block'


def rms_norm(x, g):
    xf = x.astype(jnp.float32)
    y = xf * lax.rsqrt(jnp.mean(xf * xf, axis=-1, keepdims=True) + EPS)
    return (y * g.astype(jnp.float32)).astype(x.dtype)


def to_col_major(t, rows):
    b, s = t.shape[0], t.shape[1]
    t = t.reshape((b, rows, GRID_W) + t.shape[2:])
    return jnp.swapaxes(t, 1, 2).reshape((b, s) + t.shape[3:])


def from_col_major(t, rows):
    b, s = t.shape[0], t.shape[1]
    t = t.reshape((b, GRID_W, rows) + t.shape[2:])
    return jnp.swapaxes(t, 1, 2).reshape((b, s) + t.shape[3:])


def flip(t):
    return jnp.flip(t, axis=1)


def mlstm_chunkwise(q, k, v, ig, lf, state):
    bsz, s, nh, dh = q.shape
    nc, L = s // MLSTM_CHUNK, MLSTM_CHUNK
    q = q.reshape(bsz, nc, L, nh, dh)
    k = (k * (dh ** -0.5)).reshape(bsz, nc, L, nh, dh)
    v = v.reshape(bsz, nc, L, nh, dh)
    ig = ig.reshape(bsz, nc, L, nh)
    lf = lf.reshape(bsz, nc, L, nh)
    cum = jnp.cumsum(lf, axis=2)
    tot = cum[:, :, -1]
    w_log = tot[:, :, None] - cum + ig
    m_loc = jnp.max(w_log, axis=2)
    w = jnp.exp(w_log - m_loc[:, :, None])
    c_loc = jnp.einsum('bclh,bclhd,bclhe->bchde', w, v, k)
    n_loc = jnp.einsum('bclh,bclhe->bche', w, k)

    def step(carry, inp):
        c_st, n_st, m_st = carry
        cl, nl, ml, g = inp
        m_new = jnp.maximum(g + m_st, ml)
        a = jnp.exp(g + m_st - m_new)
        bb = jnp.exp(ml - m_new)
        c_new = a[..., None, None] * c_st + bb[..., None, None] * cl
        n_new = a[..., None] * n_st + bb[..., None] * nl
        return (c_new, n_new, m_new), (c_st, n_st, m_st)

    xs = (jnp.moveaxis(c_loc, 1, 0), jnp.moveaxis(n_loc, 1, 0),
          jnp.moveaxis(m_loc, 1, 0), jnp.moveaxis(tot, 1, 0))
    final, prev = lax.scan(step, state, xs)
    c_prev, n_prev, m_prev = [jnp.moveaxis(t, 0, 1) for t in prev]

    causal = jnp.tril(jnp.ones((L, L), dtype=bool))
    log_d = cum[:, :, :, None, :] - cum[:, :, None, :, :] + ig[:, :, None, :, :]
    log_d = jnp.where(causal[None, None, :, :, None], log_d, -jnp.inf)
    log_inter = cum + m_prev[:, :, None, :]
    m_row = jnp.maximum(log_inter, jnp.max(log_d, axis=3))
    dmat = jnp.exp(log_d - m_row[:, :, :, None, :])
    w_inter = jnp.exp(log_inter - m_row)
    qk = jnp.einsum('bcjhd,bcshd->bcjsh', q, k) * dmat
    num = (jnp.einsum('bcjsh,bcshd->bcjhd', qk, v)
           + w_inter[..., None] * jnp.einsum('bchde,bcjhe->bcjhd', c_prev, q))
    den = jnp.sum(qk, axis=3) + w_inter * jnp.einsum('bche,bcjhe->bcjh', n_prev, q)
    h = num / jnp.maximum(jnp.abs(den), jnp.exp(-m_row))[..., None]
    return h.reshape(bsz, s, nh, dh), final


def mlstm_bidir(q, k, v, ig_f, lf_f, ig_b, lf_b, state_f, state_b):
    h_f, fin_f = mlstm_chunkwise(q, k, v, ig_f, lf_f, state_f)
    h_b, fin_b = mlstm_chunkwise(flip(q), flip(k), flip(v), flip(ig_b), flip(lf_b), state_b)
    return h_f + flip(h_b), fin_f, fin_b


def lru_combine(left, right):
    return (left[0] * right[0], right[0] * left[1] + right[1])


def rglru_direction(xc, w_a, b_a, w_x, b_x, lam, h0):
    bsz, t, _ = xc.shape
    xb = xc.reshape(bsz, t, N_LRU_BLOCKS, LRU_BLOCK)
    r = jax.nn.sigmoid((jnp.einsum('btnc,ncd->btnd', xb, w_a).reshape(bsz, t, D_LRU) + b_a).astype(jnp.float32))
    i = jax.nn.sigmoid((jnp.einsum('btnc,ncd->btnd', xb, w_x).reshape(bsz, t, D_LRU) + b_x).astype(jnp.float32))
    log_a = -LRU_C * r * jax.nn.softplus(-lam.astype(jnp.float32))
    a = jnp.exp(log_a)
    u = jnp.sqrt(-jnp.expm1(2.0 * log_a)) * (i * xc.astype(jnp.float32))
    a_cum, b_cum = lax.associative_scan(lru_combine, (a, u), axis=1)
    h = b_cum + a_cum * h0[:, None, :]
    return h, h[:, -1]


def short_conv(xr, w, b):
    y = lax.conv_general_dilated(xr, w[:, None, :].astype(xr.dtype), (1,), [CONV_PAD],
                                 dimension_numbers=('NWC', 'WIO', 'NWC'),
                                 feature_group_count=xr.shape[-1])
    return y + b


def zero_state(bsz):
    f32 = jnp.float32
    m_state = (jnp.zeros((bsz, N_MLSTM_HEADS, MLSTM_HEAD_DIM, MLSTM_HEAD_DIM), f32),
               jnp.zeros((bsz, N_MLSTM_HEADS, MLSTM_HEAD_DIM), f32),
               jnp.zeros((bsz, N_MLSTM_HEADS), f32))
    h_state = jnp.zeros((bsz, D_LRU), f32)
    return (m_state, m_state, h_state, h_state)


def token_mixer(u, rows, state, w_in, b_gates, mh_g, conv_w, conv_b,
                lru_w_a, lru_b_a, lru_w_x, lru_b_x, lru_lam, w_out):
    bsz, t = u.shape[0], u.shape[1]
    proj = u @ w_in
    o1 = D_MLSTM
    o4 = 4 * D_MLSTM
    o5 = o4 + N_GATE_COLS
    o6 = o5 + D_LRU
    q, k, v, o_pre = proj[..., :o1], proj[..., o1:2 * o1], proj[..., 2 * o1:3 * o1], proj[..., 3 * o1:o4]
    gates, xr, gr = proj[..., o4:o5], proj[..., o5:o6], proj[..., o6:]

    heads = lambda z: z.reshape(bsz, t, N_MLSTM_HEADS, MLSTM_HEAD_DIM)
    g = gates.astype(jnp.float32).reshape(bsz, t, 4, N_MLSTM_HEADS) + b_gates.astype(jnp.float32)
    seq = (heads(q), heads(k), heads(v), g[:, :, 0], jax.nn.log_sigmoid(g[:, :, 1]),
           g[:, :, 2], jax.nn.log_sigmoid(g[:, :, 3]))
    if rows is not None:
        seq = tuple(to_col_major(z, rows) for z in seq)
    h_m, fin_mf, fin_mb = mlstm_bidir(*seq, state[0], state[1])
    if rows is not None:
        h_m = from_col_major(h_m, rows)
    h_m = h_m * lax.rsqrt(jnp.mean(h_m * h_m, axis=-1, keepdims=True) + EPS)
    y_m = jax.nn.sigmoid(o_pre.astype(jnp.float32)) * (h_m.reshape(bsz, t, D_MLSTM) * mh_g.astype(jnp.float32))

    xc = short_conv(xr, conv_w, conv_b)
    h_f, fin_rf = rglru_direction(xc, lru_w_a[0], lru_b_a[0], lru_w_x[0], lru_b_x[0], lru_lam[0], state[2])
    h_b, fin_rb = rglru_direction(flip(xc), lru_w_a[1], lru_b_a[1], lru_w_x[1], lru_b_x[1], lru_lam[1], state[3])
    y_r = jax.nn.gelu(gr.astype(jnp.float32)) * (h_f + flip(h_b))

    y = jnp.concatenate([y_m, y_r], axis=-1).astype(u.dtype) @ w_out
    return y, (fin_mf, fin_mb, fin_rf, fin_rb)


def swiglu(u, w_ffn_in, w_ffn_out):
    gu = u @ w_ffn_in
    return (jax.nn.silu(gu[..., :D_FF]) * gu[..., D_FF:]) @ w_ffn_out


def setup_inputs(seed: int = 0) -> dict:
    key = jax.random.key(seed)
    ks = jax.random.split(key, 24)
    nrm = jax.random.normal
    f32 = jnp.float32
    x = nrm(ks[0], (BATCH, SEQ, D_MODEL), f32)
    c = nrm(ks[1], (BATCH, D_MODEL), f32)
    ctx = nrm(ks[2], (BATCH, CTX_LEN, D_MODEL), f32)
    c_ctx = nrm(ks[3], (D_MODEL,), f32)
    w_mod = nrm(ks[4], (DEPTH, D_MODEL, 6 * D_MODEL), f32) * (0.5 * D_MODEL ** -0.5)
    b_mod = nrm(ks[5], (DEPTH, 6 * D_MODEL), f32) * 0.02
    gains = 1.0 + 0.05 * nrm(ks[6], (4, DEPTH, D_MODEL), f32)
    w_in = nrm(ks[7], (DEPTH, D_MODEL, P_IN), f32) * D_MODEL ** -0.5
    f_bias = jnp.linspace(3.0, 6.0, N_MLSTM_HEADS)
    i_bias = jnp.full((N_MLSTM_HEADS,), -1.0)
    b_gates = (jnp.stack([i_bias, f_bias, i_bias, f_bias])[None]
               + 0.1 * nrm(ks[8], (DEPTH, 4, N_MLSTM_HEADS), f32))
    mh_norm_g = 1.0 + 0.05 * nrm(ks[9], (DEPTH, D_MLSTM), f32)
    conv_w = nrm(ks[10], (DEPTH, CONV_W, D_LRU), f32) * CONV_W ** -0.5
    conv_b = nrm(ks[11], (DEPTH, D_LRU), f32) * 0.02
    lru_w_a = nrm(ks[12], (DEPTH, 2, N_LRU_BLOCKS, LRU_BLOCK, LRU_BLOCK), f32) * LRU_BLOCK ** -0.5
    lru_b_a = nrm(ks[13], (DEPTH, 2, D_LRU), f32) * 0.1
    lru_w_x = nrm(ks[14], (DEPTH, 2, N_LRU_BLOCKS, LRU_BLOCK, LRU_BLOCK), f32) * LRU_BLOCK ** -0.5
    lru_b_x = nrm(ks[15], (DEPTH, 2, D_LRU), f32) * 0.1
    a0 = jax.random.uniform(ks[16], (DEPTH, 2, D_LRU), f32, minval=0.9, maxval=0.999)
    lru_lambda = jnp.log(a0) - jnp.log1p(-a0)
    w_out = nrm(ks[17], (DEPTH, D_MIX, D_MODEL), f32) * D_MIX ** -0.5
    w_ffn_in = nrm(ks[18], (DEPTH, D_MODEL, 2 * D_FF), f32) * D_MODEL ** -0.5
    w_ffn_out = nrm(ks[19], (DEPTH, D_FF, D_MODEL), f32) * D_FF ** -0.5
    return {'x': x, 'c': c, 'ctx': ctx, 'c_ctx': c_ctx, 'w_mod': w_mod, 'b_mod': b_mod,
            'g_pre_mix': gains[0], 'g_post_mix': gains[1], 'g_pre_ffn': gains[2], 'g_post_ffn': gains[3],
            'w_in': w_in, 'b_gates': b_gates, 'mh_norm_g': mh_norm_g, 'conv_w': conv_w, 'conv_b': conv_b,
            'lru_w_a': lru_w_a, 'lru_b_a': lru_b_a, 'lru_w_x': lru_w_x, 'lru_b_x': lru_b_x,
            'lru_lambda': lru_lambda, 'w_out': w_out, 'w_ffn_in': w_ffn_in, 'w_ffn_out': w_ffn_out}


def reference(x, c, ctx, c_ctx, w_mod, b_mod, g_pre_mix, g_post_mix, g_pre_ffn, g_post_ffn,
              w_in, b_gates, mh_norm_g, conv_w, conv_b, lru_w_a, lru_b_a, lru_w_x, lru_b_x,
              lru_lambda, w_out, w_ffn_in, w_ffn_out):
    bsz = x.shape[0]
    rows = x.shape[1] // GRID_W
    for layer in range(DEPTH):
        mix_params = (w_in[layer], b_gates[layer], mh_norm_g[layer], conv_w[layer], conv_b[layer],
                      lru_w_a[layer], lru_b_a[layer], lru_w_x[layer], lru_b_x[layer],
                      lru_lambda[layer], w_out[layer])
        m_lat = jnp.split((jax.nn.silu(c) @ w_mod[layer] + b_mod[layer])[:, None, :], 6, axis=-1)
        m_ctx = jnp.split((jax.nn.silu(c_ctx) @ w_mod[layer] + b_mod[layer])[None, None, :], 6, axis=-1)

        hc = rms_norm(ctx, g_pre_mix[layer]) * (1.0 + m_ctx[1]) + m_ctx[0]
        yc, ctx_state = token_mixer(hc, None, zero_state(bsz), *mix_params)

        hx = rms_norm(x, g_pre_mix[layer]) * (1.0 + m_lat[1]) + m_lat[0]
        yx, _ = token_mixer(hx, rows, ctx_state, *mix_params)
        x = x + m_lat[2] * rms_norm(yx, g_post_mix[layer])
        hx = rms_norm(x, g_pre_ffn[layer]) * (1.0 + m_lat[4]) + m_lat[3]
        x = x + m_lat[5] * rms_norm(swiglu(hx, w_ffn_in[layer], w_ffn_out[layer]), g_post_ffn[layer])

        if layer + 1 < DEPTH:
            ctx = ctx + m_ctx[2] * rms_norm(yc, g_post_mix[layer])
            hc = rms_norm(ctx, g_pre_ffn[layer]) * (1.0 + m_ctx[4]) + m_ctx[3]
            ctx = ctx + m_ctx[5] * rms_norm(swiglu(hc, w_ffn_in[layer], w_ffn_out[layer]), g_post_ffn[layer])
    return x
```

```python
import functools

import jax
import jax.numpy as jnp
from jax import lax
from jax.experimental import pallas as pl
from jax.experimental.pallas import tpu as pltpu

F32 = jnp.float32
BF16 = jnp.bfloat16

GRID_W = 64
N_HEADS = 4
CHUNK = 128
LRU_C = 8.0
EPS = 1e-6
N_GATE_TYPES = 4

LANES = 128
SUBLANES = 8
VMEM_LIMIT_BYTES = 56 * 1024 * 1024


def _cparams(sem):
    return pltpu.CompilerParams(dimension_semantics=sem,
                                vmem_limit_bytes=VMEM_LIMIT_BYTES)


def _rms(xf, g):
    return xf * lax.rsqrt(jnp.mean(xf * xf, axis=-1, keepdims=True) + EPS) * g


def _sigmoid(x):
    return 1.0 / (1.0 + jnp.exp(-x))


def _softplus(x):
    return jnp.maximum(x, 0.0) + jnp.log1p(jnp.exp(-jnp.abs(x)))


def _dot(a, b):
    return jnp.dot(a, b, preferred_element_type=F32)


def _mod_kernel(c_ref, w_ref, b_ref, o_ref):
    cv = c_ref[...]
    o_ref[...] = _dot(cv * _sigmoid(cv), w_ref[...]) + b_ref[...]


def _modulation(c_rows, w_mod, b_mod, tn=512):
    r, d = c_rows.shape
    n = w_mod.shape[1]
    return pl.pallas_call(
        _mod_kernel,
        grid=(n // tn,),
        in_specs=[pl.BlockSpec((r, d), lambda j: (0, 0)),
                  pl.BlockSpec((d, tn), lambda j: (0, j)),
                  pl.BlockSpec((1, tn), lambda j: (0, j))],
        out_specs=pl.BlockSpec((r, tn), lambda j: (0, j)),
        out_shape=jax.ShapeDtypeStruct((r, n), F32),
        compiler_params=_cparams(("arbitrary",)),
        name="mod",
    )(c_rows, w_mod, b_mod.reshape(1, n))


def _inproj_kernel(x_ref, g_ref, shift_ref, scale_ref, w_ref, wg_ref,
                   qkv_ref, wide_ref, gate_ref, hx_ref, *, n_bf16_tiles):
    j = pl.program_id(1)

    @pl.when(j == 0)
    def _():
        h = _rms(x_ref[...], g_ref[...]) * (1.0 + scale_ref[0]) + shift_ref[0]
        hb = h.astype(BF16)
        hx_ref[...] = hb
        gate_ref[...] = _dot(hb, wg_ref[...])

    @pl.when(j < n_bf16_tiles)
    def _():
        qkv_ref[...] = _dot(hx_ref[...], w_ref[...]).astype(BF16)

    @pl.when(j >= n_bf16_tiles)
    def _():
        wide_ref[...] = _dot(hx_ref[...], w_ref[...])


def _inproj(x2d, g, mod3, mod_row, w_big, w_gate, n_bf16, tm, tn):
    n, d = x2d.shape
    n_tot = w_big.shape[1]
    n_bt = n_bf16 // tn
    n_wt = (n_tot - n_bf16) // tn
    kern = functools.partial(_inproj_kernel, n_bf16_tiles=n_bt)
    return pl.pallas_call(
        kern,
        grid=(n // tm, n_bt + n_wt),
        in_specs=[pl.BlockSpec((tm, d), lambda i, j: (i, 0)),
                  pl.BlockSpec((1, d), lambda i, j: (0, 0)),
                  pl.BlockSpec((1, 1, d), lambda i, j: (mod_row(i), 0, 0)),
                  pl.BlockSpec((1, 1, d), lambda i, j: (mod_row(i), 0, 1)),
                  pl.BlockSpec((d, tn), lambda i, j: (0, j)),
                  pl.BlockSpec((d, 2 * LANES), lambda i, j: (0, 0))],
        out_specs=[pl.BlockSpec((tm, tn), lambda i, j: (i, jnp.minimum(j, n_bt - 1))),
                   pl.BlockSpec((tm, tn), lambda i, j: (i, jnp.maximum(j - n_bt, 0))),
                   pl.BlockSpec((tm, 2 * LANES), lambda i, j: (i, 0))],
        out_shape=[jax.ShapeDtypeStruct((n, n_bf16), BF16),
                   jax.ShapeDtypeStruct((n, n_tot - n_bf16), F32),
                   jax.ShapeDtypeStruct((n, 2 * LANES), F32)],
        scratch_shapes=[pltpu.VMEM((tm, d), BF16)],
        compiler_params=_cparams(("arbitrary", "arbitrary")),
        name="inproj",
    )(x2d, g.reshape(1, d), mod3, mod3, w_big, w_gate)


def _keep_mask(n, rev):
    r = lax.broadcasted_iota(jnp.int32, (n, n), 0)
    c = lax.broadcasted_iota(jnp.int32, (n, n), 1)
    return (c >= r) if rev else (c <= r)


def _split3(x):
    hi = x.astype(BF16)
    r1 = x - hi.astype(F32)
    mid = r1.astype(BF16)
    lo = (r1 - mid.astype(F32)).astype(BF16)
    return hi, mid, lo


def _gate_prep(gi, gf, bi, bf, rev):
    n = gi.shape[0]
    ig = gi + bi
    xf = gf + bf
    lf = jnp.minimum(xf, 0.0) - jnp.log1p(jnp.exp(-jnp.abs(xf)))
    tri = _keep_mask(n, rev).astype(BF16)
    tri_t = _keep_mask(n, not rev).astype(BF16)
    h, m, l = _split3(lf)
    cum = _dot(tri, h) + _dot(tri, m) + _dot(tri, l)
    h, m, l = _split3(lf.T)
    cum_t = _dot(h, tri_t) + _dot(m, tri_t) + _dot(l, tri_t)
    return cum, ig - cum, ig.T - cum_t


def _lane_col(x, lane_idx, lane):
    return jnp.sum(jnp.where(lane_idx == lane, x, 0.0), axis=1, keepdims=True)


def _mlstm_state_update(k, v, w_log, tot, h, ct_ref, n_ref, m_ref):
    m_prev = m_ref[h, 0:1, 0:1]
    m_loc = jnp.max(w_log, axis=0, keepdims=True)
    w = jnp.exp(w_log - m_loc)
    m_new = jnp.maximum(tot + m_prev, m_loc)
    a = jnp.exp(tot + m_prev - m_new)
    bb = jnp.exp(m_loc - m_new)
    wv = (w * v.astype(F32)).astype(BF16)
    ct_loc = lax.dot_general(k, wv, (((0,), (0,)), ((), ())), preferred_element_type=F32)
    n_loc = jnp.sum(w * k.astype(F32), axis=0, keepdims=True)
    ct_ref[h] = a * ct_ref[h] + bb * ct_loc
    n_ref[h] = jnp.broadcast_to(a * n_ref[h, 0:1, :] + bb * n_loc, n_ref.shape[1:])
    m_ref[h] = jnp.broadcast_to(m_new, m_ref.shape[1:])


def _mlstm_head(q, k, v, cum_c, igc_c, igc_r, keep, rev, h, ct_ref, n_ref, m_ref):
    n = q.shape[0]
    m_prev = m_ref[h, 0:1, 0:1]
    tot = cum_c[0:1, :] if rev else cum_c[n - 1:n, :]
    log_d = jnp.where(keep, cum_c + igc_r, -jnp.inf)
    log_inter = cum_c + m_prev
    m_row = jnp.maximum(log_inter, jnp.max(log_d, axis=1, keepdims=True))
    dmat = jnp.exp(log_d - m_row)
    w_inter = jnp.exp(log_inter - m_row)
    s = lax.dot_general(q, k, (((1,), (1,)), ((), ())), preferred_element_type=F32)
    qk = s * dmat
    num = _dot(qk.astype(BF16), v) + w_inter * _dot(q, ct_ref[h].astype(BF16))
    qn = jnp.sum(q.astype(F32) * n_ref[h, 0:1, :], axis=1, keepdims=True)
    den = jnp.sum(qk, axis=1, keepdims=True) + w_inter * qn
    hout = num * (1.0 / jnp.maximum(jnp.abs(den), jnp.exp(-m_row)))
    _mlstm_state_update(k, v, tot + igc_c, tot, h, ct_ref, n_ref, m_ref)
    return hout


def _mlstm_lat_kernel(qkv_ref, g_ref, oa_ref, ob_ref, bi_ref, bf_ref, mhg_ref,
                      c0_ref, n0_ref, m0_ref, y_ref,
                      ct_ref, n_ref, m_ref, hf_ref, *, nc, dh, dm):
    p = pl.program_id(1)
    t = pl.program_id(2)
    half = CHUNK // 2
    w3 = 3 * dm
    k_scale = dh ** -0.5

    @pl.when(t == 0)
    def _():
        ct_ref[...] = c0_ref[0, 0]
        n_ref[...] = n0_ref[0, 0]
        m_ref[...] = m0_ref[0, 0]

    def pair(ref, off, width):
        return jnp.concatenate([ref[0, :, off:off + width],
                                ref[0, :, ref.shape[2] // 2 + off:ref.shape[2] // 2 + off + width]],
                               axis=0)

    def run(rev):
        gi = pair(g_ref, 0, LANES)
        gf = pair(g_ref, LANES, LANES)
        cum, igc, igc_t = _gate_prep(gi, gf, bi_ref[...], bf_ref[...], rev)
        keep = _keep_mask(CHUNK, rev)
        lane_idx = lax.broadcasted_iota(jnp.int32, (CHUNK, LANES), 1)
        for h in range(N_HEADS):
            lane = h + N_HEADS * int(rev)
            hs = slice(h * dh, (h + 1) * dh)
            q = pair(qkv_ref, h * dh, dh)
            k = pair(qkv_ref, dm + h * dh, dh) * k_scale
            v = pair(qkv_ref, 2 * dm + h * dh, dh)
            hout = _mlstm_head(q, k, v, _lane_col(cum, lane_idx, lane),
                               _lane_col(igc, lane_idx, lane), igc_t[lane:lane + 1, :],
                               keep, rev, h, ct_ref, n_ref, m_ref)
            if not rev:
                hf_ref[t, :, hs] = hout
            else:
                hsum = hout + hf_ref[nc - 1 - t, :, hs]
                hn = hsum * lax.rsqrt(jnp.mean(hsum * hsum, axis=1, keepdims=True) + EPS)
                o = jnp.concatenate([oa_ref[0, :, hs], ob_ref[0, :, hs]], axis=0)
                ym = (_sigmoid(o) * (hn * mhg_ref[:, hs])).astype(BF16)
                y_ref[0, :, hs] = ym[:half]
                y_ref[0, :, dm + h * dh:dm + (h + 1) * dh] = ym[half:]

    @pl.when(p == 0)
    def _():
        run(False)

    @pl.when(p == 1)
    def _():
        run(True)


def _mlstm_ctx_kernel(qkv_ref, g_ref, bi_ref, bf_ref, c_out, n_out, m_out,
                      ct_ref, n_ref, m_ref, *, nc, dh, dm):
    p = pl.program_id(1)
    t = pl.program_id(2)
    k_scale = dh ** -0.5

    @pl.when(t == 0)
    def _():
        ct_ref[...] = jnp.zeros_like(ct_ref)
        n_ref[...] = jnp.zeros_like(n_ref)
        m_ref[...] = jnp.zeros_like(m_ref)

    def run(rev):
        cum, igc, _ = _gate_prep(g_ref[0, :, 0:LANES], g_ref[0, :, LANES:2 * LANES],
                                 bi_ref[...], bf_ref[...], rev)
        lane_idx = lax.broadcasted_iota(jnp.int32, (CHUNK, LANES), 1)
        for h in range(N_HEADS):
            lane = h + N_HEADS * int(rev)
            k = qkv_ref[0, :, dm + h * dh:dm + (h + 1) * dh] * k_scale
            v = qkv_ref[0, :, 2 * dm + h * dh:2 * dm + (h + 1) * dh]
            cum_c = _lane_col(cum, lane_idx, lane)
            tot = cum_c[0:1, :] if rev else cum_c[CHUNK - 1:CHUNK, :]
            _mlstm_state_update(k, v, tot + _lane_col(igc, lane_idx, lane), tot,
                                h, ct_ref, n_ref, m_ref)

    @pl.when(p == 0)
    def _():
        run(False)

    @pl.when(p == 1)
    def _():
        run(True)

    @pl.when(t == nc - 1)
    def _():
        c_out[0, 0] = ct_ref[...]
        n_out[0, 0] = n_ref[...]
        m_out[0, 0] = m_ref[...]


def _mlstm_ctx(qkv, gates, bi, bf, dh):
    bsz, t, w3 = qkv.shape
    dm = w3 // 3
    nc = t // CHUNK
    chunk = lambda p, s: s + p * (nc - 1 - 2 * s)
    kern = functools.partial(_mlstm_ctx_kernel, nc=nc, dh=dh, dm=dm)
    st = lambda shape: pl.BlockSpec((1, 1) + shape, lambda b, p, s: (b, p) + (0,) * len(shape))
    return pl.pallas_call(
        kern,
        grid=(bsz, 2, nc),
        in_specs=[pl.BlockSpec((1, CHUNK, w3), lambda b, p, s: (b, chunk(p, s), 0)),
                  pl.BlockSpec((1, CHUNK, 2 * LANES), lambda b, p, s: (b, chunk(p, s), 0)),
                  pl.BlockSpec((1, LANES), lambda b, p, s: (0, 0)),
                  pl.BlockSpec((1, LANES), lambda b, p, s: (0, 0))],
        out_specs=[st((N_HEADS, dh, dh)), st((N_HEADS, SUBLANES, dh)),
                   st((N_HEADS, SUBLANES, LANES))],
        out_shape=[jax.ShapeDtypeStruct((bsz, 2, N_HEADS, dh, dh), F32),
                   jax.ShapeDtypeStruct((bsz, 2, N_HEADS, SUBLANES, dh), F32),
                   jax.ShapeDtypeStruct((bsz, 2, N_HEADS, SUBLANES, LANES), F32)],
        scratch_shapes=[pltpu.VMEM((N_HEADS, dh, dh), F32),
                        pltpu.VMEM((N_HEADS, SUBLANES, dh), F32),
                        pltpu.VMEM((N_HEADS, SUBLANES, LANES), F32)],
        compiler_params=_cparams(("arbitrary",) * 3),
        name="mlstm_ctx",
    )(qkv, gates, bi, bf)


def _mlstm_lat(qkv, gates, wide, bi, bf, mh_g, c0, n0, m0, dh):
    bsz, s, w3 = qkv.shape
    dm = w3 // 3
    rows = s // GRID_W
    cols_per_chunk = CHUNK // rows
    nc = GRID_W // cols_per_chunk
    assert cols_per_chunk == 2 and wide.shape[2] == w3
    chunk = lambda p, t: t + p * (nc - 1 - 2 * t)
    ochunk = lambda p, t: nc - 1 - p * t
    kern = functools.partial(_mlstm_lat_kernel, nc=nc, dh=dh, dm=dm)
    st = lambda shape: pl.BlockSpec((1, 1) + shape, lambda b, p, t: (b, p) + (0,) * len(shape))
    wide_v = wide.reshape(bsz, rows, GRID_W * w3)
    y = pl.pallas_call(
        kern,
        grid=(bsz, 2, nc),
        in_specs=[pl.BlockSpec((1, rows, 2 * w3), lambda b, p, t: (b, 0, chunk(p, t))),
                  pl.BlockSpec((1, rows, 4 * LANES), lambda b, p, t: (b, 0, chunk(p, t))),
                  pl.BlockSpec((1, rows, dm), lambda b, p, t: (b, 0, 6 * ochunk(p, t))),
                  pl.BlockSpec((1, rows, dm), lambda b, p, t: (b, 0, 6 * ochunk(p, t) + 3)),
                  pl.BlockSpec((1, LANES), lambda b, p, t: (0, 0)),
                  pl.BlockSpec((1, LANES), lambda b, p, t: (0, 0)),
                  pl.BlockSpec((1, dm), lambda b, p, t: (0, 0)),
                  st((N_HEADS, dh, dh)), st((N_HEADS, SUBLANES, dh)),
                  st((N_HEADS, SUBLANES, LANES))],
        out_specs=pl.BlockSpec((1, rows, 2 * dm), lambda b, p, t: (b, 0, ochunk(p, t))),
        out_shape=jax.ShapeDtypeStruct((bsz, rows, GRID_W * dm), BF16),
        scratch_shapes=[pltpu.VMEM((N_HEADS, dh, dh), F32),
                        pltpu.VMEM((N_HEADS, SUBLANES, dh), F32),
                        pltpu.VMEM((N_HEADS, SUBLANES, LANES), F32),
                        pltpu.VMEM((nc, CHUNK, dm), F32)],
        compiler_params=_cparams(("arbitrary",) * 3),
        name="mlstm_lat",
    )(qkv.reshape(bsz, rows, GRID_W * w3), gates.reshape(bsz, rows, GRID_W * 2 * LANES),
      wide_v, wide_v, bi, bf, mh_g.reshape(1, dm), c0, n0, m0)
    return y.reshape(bsz, s, dm)


_LRU_TILE = 512
_CONV_PAD_ROWS = SUBLANES


def _gelu_tanh(x):
    return x * (0.5 * (1.0 + jnp.tanh(0.7978845608028654 * (x + 0.044715 * (x * x * x)))))


def _rglru_kernel(xr_ref, gr_ref, xctx_ref, cw_ref, cb_ref, w4_ref, b4_ref, lam_ref, y_ref,
                  xp_ref, af_ref, uf_ref, ab_ref, ub_ref, *, t_lat, t_ctx):
    blk = LANES
    cw = cw_ref[...]
    cb = cb_ref[...]
    w4 = w4_ref[0]
    b4 = b4_ref[0]
    sp = _softplus(-lam_ref[...])
    row8 = lax.broadcasted_iota(jnp.int32, (SUBLANES, blk), 0)
    pad = _CONV_PAD_ROWS

    def a_u(r_pre, i_pre, xc, sp_row):
        log_a = (-LRU_C * _sigmoid(r_pre)) * sp_row
        a = jnp.exp(log_a)
        one_minus_a2 = -jnp.tanh(log_a) * (a * a + 1.0)
        return a, jnp.sqrt(one_minus_a2) * (_sigmoid(i_pre) * xc)

    def fill(src_ref, t_len, tile):
        zeros = jnp.zeros((pad, blk), F32)
        xp_ref[0:pad, :] = zeros
        xp_ref[pl.ds(pad + t_len, pad), :] = zeros

        def copy(i, c):
            r0 = pl.multiple_of(i * tile, tile)
            xp_ref[pl.ds(pad + r0, tile), :] = src_ref[0, pl.ds(r0, tile), :]
            return c
        lax.fori_loop(0, t_len // tile, copy, 0)

        def body(i, c):
            r0 = pl.multiple_of(i * tile, tile)
            ext = xp_ref[pl.ds(r0, tile + 2 * pad), :]
            xc = (cw[0:1] * ext[pad - 2:pad - 2 + tile] + cw[1:2] * ext[pad - 1:pad - 1 + tile]
                  + cw[2:3] * ext[pad:pad + tile] + cw[3:4] * ext[pad + 1:pad + 1 + tile]) + cb
            z = _dot(xc.astype(BF16), w4) + b4
            a, u = a_u(z[:, 0:blk], z[:, blk:2 * blk], xc, sp[0:1])
            af_ref[pl.ds(r0, tile), :] = a
            uf_ref[pl.ds(r0, tile), :] = u
            a, u = a_u(z[:, 2 * blk:3 * blk], z[:, 3 * blk:4 * blk], xc, sp[1:2])
            ab_ref[pl.ds(r0, tile), :] = a
            ub_ref[pl.ds(r0, tile), :] = u
            return c
        lax.fori_loop(0, t_len // tile, body, 0)

    def scan(t_len, h0f, h0b):
        nb = t_len // SUBLANES

        def body(i, carry):
            cf, cbw = carry
            rf = pl.multiple_of(i * SUBLANES, SUBLANES)
            rb = pl.multiple_of((nb - 1 - i) * SUBLANES, SUBLANES)
            a = af_ref[pl.ds(rf, SUBLANES), :]
            u = uf_ref[pl.ds(rf, SUBLANES), :]
            for d in (1, 2, 4):
                m = row8 >= d
                u = a * jnp.where(m, pltpu.roll(u, d, 0), 0.0) + u
                a = a * jnp.where(m, pltpu.roll(a, d, 0), 1.0)
            hf = u + a * cf
            uf_ref[pl.ds(rf, SUBLANES), :] = hf
            a = ab_ref[pl.ds(rb, SUBLANES), :]
            u = ub_ref[pl.ds(rb, SUBLANES), :]
            for d in (1, 2, 4):
                m = row8 < SUBLANES - d
                u = a * jnp.where(m, pltpu.roll(u, SUBLANES - d, 0), 0.0) + u
                a = a * jnp.where(m, pltpu.roll(a, SUBLANES - d, 0), 1.0)
            hb = u + a * cbw
            ub_ref[pl.ds(rb, SUBLANES), :] = hb
            return hf[SUBLANES - 1:SUBLANES, :], hb[0:1, :]
        return lax.fori_loop(0, nb, body, (h0f, h0b), unroll=4)

    zero = jnp.zeros((1, blk), F32)
    fill(xctx_ref, t_ctx, t_ctx)
    h0f, h0b = scan(t_ctx, zero, zero)
    fill(xr_ref, t_lat, _LRU_TILE)
    scan(t_lat, h0f, h0b)

    def emit(i, c):
        r0 = pl.multiple_of(i * _LRU_TILE, _LRU_TILE)
        g = gr_ref[0, pl.ds(r0, _LRU_TILE), :]
        hsum = uf_ref[pl.ds(r0, _LRU_TILE), :] + ub_ref[pl.ds(r0, _LRU_TILE), :]
        y_ref[0, pl.ds(r0, _LRU_TILE), :] = (_gelu_tanh(g) * hsum).astype(BF16)
        return c
    lax.fori_loop(0, t_lat // _LRU_TILE, emit, 0)


def _rglru(wide, wide_ctx, conv_w, conv_b, w4, b4, lam, dm):
    bsz, s, _ = wide.shape
    t_ctx = wide_ctx.shape[1]
    nblk = dm // LANES
    kern = functools.partial(_rglru_kernel, t_lat=s, t_ctx=t_ctx)
    col = lambda shape, off: pl.BlockSpec(shape, lambda b, n: (b, 0, off + n))
    par = lambda shape: pl.BlockSpec(shape, lambda b, n: (0,) * (len(shape) - 1) + (n,))
    return pl.pallas_call(
        kern,
        grid=(bsz, nblk),
        in_specs=[col((1, s, LANES), nblk), col((1, s, LANES), 2 * nblk),
                  col((1, t_ctx, LANES), nblk),
                  par((conv_w.shape[0], LANES)), par((1, LANES)),
                  pl.BlockSpec((1, LANES, 4 * LANES), lambda b, n: (n, 0, 0)),
                  pl.BlockSpec((1, 1, 4 * LANES), lambda b, n: (n, 0, 0)),
                  par((2, LANES))],
        out_specs=pl.BlockSpec((1, s, LANES), lambda b, n: (b, 0, n)),
        out_shape=jax.ShapeDtypeStruct((bsz, s, dm), BF16),
        scratch_shapes=[pltpu.VMEM((s + 2 * _CONV_PAD_ROWS, LANES), F32)]
        + [pltpu.VMEM((s, LANES), F32)] * 4,
        compiler_params=_cparams(("arbitrary", "arbitrary")),
        name="rglru",
    )(wide, wide, wide_ctx, conv_w, conv_b.reshape(1, dm), w4, b4, lam)


def _outproj_kernel(ym_ref, yr_ref, w_ref, x_ref, gpost_ref, gpre_ref,
                    gate_ref, scale_ref, shift_ref, x1_ref, h2_ref, *, dm):
    y = _dot(ym_ref[...], w_ref[0:dm, :]) + _dot(yr_ref[...], w_ref[dm:2 * dm, :])
    x1 = x_ref[...] + gate_ref[0] * _rms(y, gpost_ref[...])
    x1_ref[...] = x1
    h2 = _rms(x1, gpre_ref[...]) * (1.0 + scale_ref[0]) + shift_ref[0]
    h2_ref[...] = h2.astype(BF16)


def _outproj(ym, yr, w_out, x2d, g_post, g_pre, mod3, rows_per_sample, tm):
    n, d = x2d.shape
    dm = ym.shape[1]
    kern = functools.partial(_outproj_kernel, dm=dm)
    row = lambda i: (i * tm) // rows_per_sample
    modspec = lambda k: pl.BlockSpec((1, 1, d), lambda i: (row(i), 0, k))
    vec = pl.BlockSpec((1, d), lambda i: (0, 0))
    return pl.pallas_call(
        kern,
        grid=(n // tm,),
        in_specs=[pl.BlockSpec((tm, dm), lambda i: (i, 0)),
                  pl.BlockSpec((tm, dm), lambda i: (i, 0)),
                  pl.BlockSpec((2 * dm, d), lambda i: (0, 0)),
                  pl.BlockSpec((tm, d), lambda i: (i, 0)),
                  vec, vec, modspec(2), modspec(4), modspec(3)],
        out_specs=[pl.BlockSpec((tm, d), lambda i: (i, 0)),
                   pl.BlockSpec((tm, d), lambda i: (i, 0))],
        out_shape=[jax.ShapeDtypeStruct((n, d), F32), jax.ShapeDtypeStruct((n, d), BF16)],
        compiler_params=_cparams(("arbitrary",)),
        name="outproj",
    )(ym, yr, w_out, x2d, g_post.reshape(1, d), g_pre.reshape(1, d), mod3, mod3, mod3)


def _ffn_in_kernel(h_ref, wg_ref, wu_ref, o_ref):
    h = h_ref[...]
    g = _dot(h, wg_ref[...])
    u = _dot(h, wu_ref[...])
    o_ref[...] = ((g * _sigmoid(g)) * u).astype(BF16)


def _ffn_in(h2, w_in, d_ff, tm, tn):
    n, d = h2.shape
    nj = d_ff // tn
    return pl.pallas_call(
        _ffn_in_kernel,
        grid=(n // tm, nj),
        in_specs=[pl.BlockSpec((tm, d), lambda i, j: (i, 0)),
                  pl.BlockSpec((d, tn), lambda i, j: (0, j)),
                  pl.BlockSpec((d, tn), lambda i, j: (0, j + nj))],
        out_specs=pl.BlockSpec((tm, tn), lambda i, j: (i, j)),
        out_shape=jax.ShapeDtypeStruct((n, d_ff), BF16),
        compiler_params=_cparams(("arbitrary", "arbitrary")),
        name="ffn_in",
    )(h2, w_in, w_in)


def _ffn_out_kernel(a_ref, w_ref, x1_ref, g_ref, gate_ref, o_ref, acc_ref):
    k = pl.program_id(1)

    @pl.when(k == 0)
    def _():
        acc_ref[...] = jnp.zeros_like(acc_ref)

    acc_ref[...] += _dot(a_ref[...], w_ref[...])

    @pl.when(k == pl.num_programs(1) - 1)
    def _():
        o_ref[...] = x1_ref[...] + gate_ref[0] * _rms(acc_ref[...], g_ref[...])


def _ffn_out(act, w_out, x1, g_post, mod3, rows_per_sample, tm, tk):
    n, d_ff = act.shape
    d = x1.shape[1]
    row = lambda i: (i * tm) // rows_per_sample
    return pl.pallas_call(
        _ffn_out_kernel,
        grid=(n // tm, d_ff // tk),
        in_specs=[pl.BlockSpec((tm, tk), lambda i, k: (i, k)),
                  pl.BlockSpec((tk, d), lambda i, k: (k, 0)),
                  pl.BlockSpec((tm, d), lambda i, k: (i, 0)),
                  pl.BlockSpec((1, d), lambda i, k: (0, 0)),
                  pl.BlockSpec((1, 1, d), lambda i, k: (row(i), 0, 5))],
        out_specs=pl.BlockSpec((tm, d), lambda i, k: (i, 0)),
        out_shape=jax.ShapeDtypeStruct((n, d), F32),
        scratch_shapes=[pltpu.VMEM((tm, d), F32)],
        compiler_params=_cparams(("arbitrary", "arbitrary")),
        name="ffn_out",
    )(act, w_out, x1, g_post.reshape(1, d), mod3)


def kernel(x, c, ctx, c_ctx, w_mod, b_mod, g_pre_mix, g_post_mix, g_pre_ffn, g_post_ffn,
           w_in, b_gates, mh_norm_g, conv_w, conv_b, lru_w_a, lru_b_a, lru_w_x, lru_b_x,
           lru_lambda, w_out, w_ffn_in, w_ffn_out):
    bsz, s, d = x.shape
    t_ctx = ctx.shape[1]
    depth = w_mod.shape[0]
    assert depth == 1, "context-stream update between layers is not implemented"
    dm = mh_norm_g.shape[1]
    dl = conv_w.shape[2]
    assert dm == dl
    dh = dm // N_HEADS
    n_gates = N_GATE_TYPES * N_HEADS
    d_ff = w_ffn_out.shape[1]
    layer = 0

    wi = w_in[layer]
    o4 = 4 * dm
    o5 = o4 + n_gates
    w_big = jnp.concatenate([wi[:, :o4], wi[:, o5:]], axis=1).astype(BF16)
    wgt = wi[:, o4:o5].reshape(d, N_GATE_TYPES, N_HEADS)
    zpad = jnp.zeros((d, LANES - 2 * N_HEADS), wi.dtype)
    w_gate = jnp.concatenate([wgt[:, 0], wgt[:, 2], zpad, wgt[:, 1], wgt[:, 3], zpad],
                             axis=1).astype(BF16)
    bg = b_gates[layer].astype(F32)
    bpad = jnp.zeros((LANES - 2 * N_HEADS,), F32)
    bi = jnp.concatenate([bg[0], bg[2], bpad]).reshape(1, LANES)
    bf = jnp.concatenate([bg[1], bg[3], bpad]).reshape(1, LANES)
    nblk = lru_w_a.shape[2]
    wa, wx = lru_w_a[layer], lru_w_x[layer]
    w4 = jnp.concatenate([wa[0], wx[0], wa[1], wx[1]], axis=2).astype(BF16)
    ba = lru_b_a[layer].reshape(2, nblk, 1, LANES)
    bx = lru_b_x[layer].reshape(2, nblk, 1, LANES)
    b4 = jnp.concatenate([ba[0], bx[0], ba[1], bx[1]], axis=2)
    w_o = w_out[layer].astype(BF16)
    w_f1 = w_ffn_in[layer].astype(BF16)
    w_f2 = w_ffn_out[layer].astype(BF16)

    n_rows = SUBLANES * ((bsz + 1 + SUBLANES - 1) // SUBLANES)
    c_rows = jnp.concatenate([c, c_ctx[None], jnp.zeros((n_rows - bsz - 1, d), c.dtype)], axis=0)
    mod3 = _modulation(c_rows, w_mod[layer], b_mod[layer]).reshape(n_rows, 1, 6 * d)

    x2d = x.reshape(bsz * s, d)
    tm = 512
    qkv, wide, gates = _inproj(x2d, g_pre_mix[layer], mod3, lambda i: (i * tm) // s,
                               w_big, w_gate, 3 * dm, tm, 1024)
    qkv_c, wide_c, gates_c = _inproj(ctx.reshape(bsz * t_ctx, d), g_pre_mix[layer], mod3,
                                     lambda i: bsz, w_big, w_gate, 3 * dm, tm, 1024)

    c0, n0, m0 = _mlstm_ctx(qkv_c.reshape(bsz, t_ctx, 3 * dm),
                            gates_c.reshape(bsz, t_ctx, 2 * LANES), bi, bf, dh)
    y_m = _mlstm_lat(qkv.reshape(bsz, s, 3 * dm), gates.reshape(bsz, s, 2 * LANES),
                     wide.reshape(bsz, s, 3 * dm), bi, bf, mh_norm_g[layer], c0, n0, m0, dh)

    y_r = _rglru(wide.reshape(bsz, s, 3 * dm), wide_c.reshape(bsz, t_ctx, 3 * dm),
                 conv_w[layer], conv_b[layer], w4, b4, lru_lambda[layer], dm)

    x1, h2 = _outproj(y_m.reshape(bsz * s, dm), y_r.reshape(bsz * s, dm), w_o, x2d,
                      g_post_mix[layer], g_pre_ffn[layer], mod3, s, 512)
    act = _ffn_in(h2, w_f1, d_ff, 1024, 512)
    out = _ffn_out(act, w_f2, x1, g_post_ffn[layer], mod3, s, 512, 512)
    return out.reshape(bsz, s, d)
```

```python
import functools

import jax
import jax.numpy as jnp
from jax import lax
from jax.experimental import pallas as pl
from jax.experimental.pallas import tpu as pltpu

F32 = jnp.float32
BF16 = jnp.bfloat16

GRID_W = 64
N_HEADS = 4
CHUNK = 128
LRU_C = 8.0
EPS = 1e-6
N_GATE_TYPES = 4

LANES = 128
SUBLANES = 8
VMEM_LIMIT_BYTES = 56 * 1024 * 1024

COLS_PER_TILE = SUBLANES


def _cparams(sem):
    return pltpu.CompilerParams(dimension_semantics=sem,
                                vmem_limit_bytes=VMEM_LIMIT_BYTES)


def _rms(xf, g):
    return xf * lax.rsqrt(jnp.mean(xf * xf, axis=-1, keepdims=True) + EPS) * g


def _sigmoid(x):
    return 1.0 / (1.0 + jnp.exp(-x))


def _sigmoid_tanh(x):
    return 0.5 * jnp.tanh(0.5 * x) + 0.5


def _softplus(x):
    return jnp.maximum(x, 0.0) + jnp.log1p(jnp.exp(-jnp.abs(x)))


def _dot(a, b):
    return jnp.dot(a, b, preferred_element_type=F32)


def _mod_kernel(c_ref, w_ref, b_ref, o_ref):
    cv = c_ref[...]
    o_ref[...] = _dot(cv * _sigmoid(cv), w_ref[...]) + b_ref[...]


def _modulation(c_rows, w_mod, b_mod, tn=512):
    r, d = c_rows.shape
    n = w_mod.shape[1]
    return pl.pallas_call(
        _mod_kernel,
        grid=(n // tn,),
        in_specs=[pl.BlockSpec((r, d), lambda j: (0, 0)),
                  pl.BlockSpec((d, tn), lambda j: (0, j)),
                  pl.BlockSpec((1, tn), lambda j: (0, j))],
        out_specs=pl.BlockSpec((r, tn), lambda j: (0, j)),
        out_shape=jax.ShapeDtypeStruct((r, n), F32),
        compiler_params=_cparams(("arbitrary",)),
        name="mod",
    )(c_rows, w_mod, b_mod.reshape(1, n))


def _grid_to_colmajor(x_blk):
    rows, ct, d = x_blk.shape
    return jnp.swapaxes(x_blk, 0, 1).reshape(ct * rows, d)


def _colmajor_to_grid(y, rows):
    n, d = y.shape
    return jnp.swapaxes(y.reshape(n // rows, rows, d), 0, 1)


def _inproj_kernel(x_ref, g_ref, shift_ref, scale_ref, w_ref, wg_ref,
                   qkv_ref, wide_ref, gate_ref, hx_ref, *scratch, n_bf16_tiles, col_tile):
    j = pl.program_id(1)

    def norm_mod(xf):
        return (_rms(xf, g_ref[...]) * (1.0 + scale_ref[0]) + shift_ref[0]).astype(BF16)

    @pl.when(j == 0)
    def _():
        if col_tile:
            xt_ref, = scratch
            xt_ref[...] = _grid_to_colmajor(x_ref[0])
            hx_ref[...] = norm_mod(xt_ref[...])
        else:
            hx_ref[...] = norm_mod(x_ref[...])
        gate_ref[...] = _dot(hx_ref[...], wg_ref[...])

    @pl.when(j < n_bf16_tiles)
    def _():
        qkv_ref[...] = _dot(hx_ref[...], w_ref[...]).astype(BF16)

    @pl.when(j >= n_bf16_tiles)
    def _():
        wide_ref[...] = _dot(hx_ref[...], w_ref[...])


def _inproj(x, g, mod3, mod_row, w_big, w_gate, n_bf16, tm, tn):
    d = x.shape[-1]
    col_tile = x.ndim == 4
    if col_tile:
        bsz, rows, cols, _ = x.shape
        n = bsz * rows * cols
        ct = tm // rows
        tiles_per_sample = cols // ct
        x_spec = pl.BlockSpec((1, rows, ct, d),
                              lambda i, j: (i // tiles_per_sample, 0, i % tiles_per_sample, 0))
    else:
        n = x.shape[0]
        x_spec = pl.BlockSpec((tm, d), lambda i, j: (i, 0))
    n_tot = w_big.shape[1]
    n_bt = n_bf16 // tn
    n_wt = (n_tot - n_bf16) // tn
    kern = functools.partial(_inproj_kernel, n_bf16_tiles=n_bt, col_tile=col_tile)
    return pl.pallas_call(
        kern,
        grid=(n // tm, n_bt + n_wt),
        in_specs=[x_spec,
                  pl.BlockSpec((1, d), lambda i, j: (0, 0)),
                  pl.BlockSpec((1, 1, d), lambda i, j: (mod_row(i), 0, 0)),
                  pl.BlockSpec((1, 1, d), lambda i, j: (mod_row(i), 0, 1)),
                  pl.BlockSpec((d, tn), lambda i, j: (0, j)),
                  pl.BlockSpec((d, 2 * LANES), lambda i, j: (0, 0))],
        out_specs=[pl.BlockSpec((tm, tn), lambda i, j: (i, jnp.minimum(j, n_bt - 1))),
                   pl.BlockSpec((tm, tn), lambda i, j: (i, jnp.maximum(j - n_bt, 0))),
                   pl.BlockSpec((tm, 2 * LANES), lambda i, j: (i, 0))],
        out_shape=[jax.ShapeDtypeStruct((n, n_bf16), BF16),
                   jax.ShapeDtypeStruct((n, n_tot - n_bf16), F32),
                   jax.ShapeDtypeStruct((n, 2 * LANES), F32)],
        scratch_shapes=[pltpu.VMEM((tm, d), BF16)] + ([pltpu.VMEM((tm, d), F32)] if col_tile else []),
        compiler_params=_cparams(("arbitrary", "arbitrary")),
        name="inproj",
    )(x, g.reshape(1, d), mod3, mod3, w_big, w_gate)


def _keep_mask(n, rev):
    r = lax.broadcasted_iota(jnp.int32, (n, n), 0)
    c = lax.broadcasted_iota(jnp.int32, (n, n), 1)
    return (c >= r) if rev else (c <= r)


def _split3(x):
    hi = x.astype(BF16)
    r1 = x - hi.astype(F32)
    mid = r1.astype(BF16)
    lo = (r1 - mid.astype(F32)).astype(BF16)
    return hi, mid, lo


def _gate_prep(gi, gf, bi, bf, rev):
    n = gi.shape[0]
    ig = gi + bi
    xf = gf + bf
    lf = jnp.minimum(xf, 0.0) - jnp.log1p(jnp.exp(-jnp.abs(xf)))
    tri = _keep_mask(n, rev).astype(BF16)
    tri_t = _keep_mask(n, not rev).astype(BF16)
    h, m, l = _split3(lf)
    cum = _dot(tri, h) + _dot(tri, m) + _dot(tri, l)
    h, m, l = _split3(lf.T)
    cum_t = _dot(h, tri_t) + _dot(m, tri_t) + _dot(l, tri_t)
    return cum, ig - cum, ig.T - cum_t


def _lane_col(x, lane_idx, lane):
    return jnp.sum(jnp.where(lane_idx == lane, x, 0.0), axis=1, keepdims=True)


def _mlstm_state_update(k, v, w_log, tot, h, ct_ref, n_ref, m_ref):
    m_prev = m_ref[h, 0:1, 0:1]
    m_loc = jnp.max(w_log, axis=0, keepdims=True)
    w = jnp.exp(w_log - m_loc)
    m_new = jnp.maximum(tot + m_prev, m_loc)
    a = jnp.exp(tot + m_prev - m_new)
    bb = jnp.exp(m_loc - m_new)
    wv = (w * v.astype(F32)).astype(BF16)
    ct_loc = lax.dot_general(k, wv, (((0,), (0,)), ((), ())), preferred_element_type=F32)
    n_loc = jnp.sum(w * k.astype(F32), axis=0, keepdims=True)
    ct_ref[h] = a * ct_ref[h] + bb * ct_loc
    n_ref[h] = jnp.broadcast_to(a * n_ref[h, 0:1, :] + bb * n_loc, n_ref.shape[1:])
    m_ref[h] = jnp.broadcast_to(m_new, m_ref.shape[1:])


def _mlstm_head(q, k, v, cum_c, igc_c, igc_r, keep, rev, h, ct_ref, n_ref, m_ref):
    n = q.shape[0]
    m_prev = m_ref[h, 0:1, 0:1]
    tot = cum_c[0:1, :] if rev else cum_c[n - 1:n, :]
    log_d = jnp.where(keep, cum_c + igc_r, -jnp.inf)
    log_inter = cum_c + m_prev
    m_row = jnp.maximum(log_inter, jnp.max(log_d, axis=1, keepdims=True))
    dmat = jnp.exp(log_d - m_row)
    w_inter = jnp.exp(log_inter - m_row)
    s = lax.dot_general(q, k, (((1,), (1,)), ((), ())), preferred_element_type=F32)
    qk = s * dmat
    num = _dot(qk.astype(BF16), v) + w_inter * _dot(q, ct_ref[h].astype(BF16))
    qn = jnp.sum(q.astype(F32) * n_ref[h, 0:1, :], axis=1, keepdims=True)
    den = jnp.sum(qk, axis=1, keepdims=True) + w_inter * qn
    hout = num * (1.0 / jnp.maximum(jnp.abs(den), jnp.exp(-m_row)))
    _mlstm_state_update(k, v, tot + igc_c, tot, h, ct_ref, n_ref, m_ref)
    return hout


def _mlstm_kernel(*refs, nc, dh, dm, emit_h):
    if emit_h:
        (qkv_ref, g_ref, o_ref, bi_ref, bf_ref, mhg_ref, c0_ref, n0_ref, m0_ref,
         y_ref, ct_ref, n_ref, m_ref, hf_ref) = refs
    else:
        (qkv_ref, g_ref, bi_ref, bf_ref, c_out, n_out, m_out, ct_ref, n_ref, m_ref) = refs
    p = pl.program_id(1)
    t = pl.program_id(2)
    k_scale = dh ** -0.5

    @pl.when(t == 0)
    def _():
        if emit_h:
            ct_ref[...] = c0_ref[0, 0]
            n_ref[...] = n0_ref[0, 0]
            m_ref[...] = m0_ref[0, 0]
        else:
            ct_ref[...] = jnp.zeros_like(ct_ref)
            n_ref[...] = jnp.zeros_like(n_ref)
            m_ref[...] = jnp.zeros_like(m_ref)

    def run(rev):
        cum, igc, igc_t = _gate_prep(g_ref[0, :, 0:LANES], g_ref[0, :, LANES:2 * LANES],
                                     bi_ref[...], bf_ref[...], rev)
        keep = _keep_mask(CHUNK, rev)
        lane_idx = lax.broadcasted_iota(jnp.int32, (CHUNK, LANES), 1)
        for h in range(N_HEADS):
            lane = h + N_HEADS * int(rev)
            hs = slice(h * dh, (h + 1) * dh)
            k = qkv_ref[0, :, dm + h * dh:dm + (h + 1) * dh] * k_scale
            v = qkv_ref[0, :, 2 * dm + h * dh:2 * dm + (h + 1) * dh]
            cum_c = _lane_col(cum, lane_idx, lane)
            igc_c = _lane_col(igc, lane_idx, lane)
            if not emit_h:
                tot = cum_c[0:1, :] if rev else cum_c[CHUNK - 1:CHUNK, :]
                _mlstm_state_update(k, v, tot + igc_c, tot, h, ct_ref, n_ref, m_ref)
                continue
            hout = _mlstm_head(qkv_ref[0, :, hs], k, v, cum_c, igc_c, igc_t[lane:lane + 1, :],
                               keep, rev, h, ct_ref, n_ref, m_ref)
            if not rev:
                hf_ref[t, :, hs] = hout
            else:
                hsum = hout + hf_ref[nc - 1 - t, :, hs]
                hn = hsum * lax.rsqrt(jnp.mean(hsum * hsum, axis=1, keepdims=True) + EPS)
                y_ref[0, :, hs] = (_sigmoid(o_ref[0, :, hs]) * (hn * mhg_ref[:, hs])).astype(BF16)

    @pl.when(p == 0)
    def _():
        run(False)

    @pl.when(p == 1)
    def _():
        run(True)

    if not emit_h:
        @pl.when(t == nc - 1)
        def _():
            c_out[0, 0] = ct_ref[...]
            n_out[0, 0] = n_ref[...]
            m_out[0, 0] = m_ref[...]


def _mlstm_state_shapes(dh):
    return [(N_HEADS, dh, dh), (N_HEADS, SUBLANES, dh), (N_HEADS, SUBLANES, LANES)]


def _mlstm_ctx(qkv, gates, bi, bf, dh):
    bsz, t, w3 = qkv.shape
    dm = w3 // 3
    nc = t // CHUNK
    chunk = lambda p, s: s + p * (nc - 1 - 2 * s)
    kern = functools.partial(_mlstm_kernel, nc=nc, dh=dh, dm=dm, emit_h=False)
    st = lambda shape: pl.BlockSpec((1, 1) + shape, lambda b, p, s: (b, p) + (0,) * len(shape))
    shapes = _mlstm_state_shapes(dh)
    return pl.pallas_call(
        kern,
        grid=(bsz, 2, nc),
        in_specs=[pl.BlockSpec((1, CHUNK, w3), lambda b, p, s: (b, chunk(p, s), 0)),
                  pl.BlockSpec((1, CHUNK, 2 * LANES), lambda b, p, s: (b, chunk(p, s), 0)),
                  pl.BlockSpec((1, LANES), lambda b, p, s: (0, 0)),
                  pl.BlockSpec((1, LANES), lambda b, p, s: (0, 0))],
        out_specs=[st(sh) for sh in shapes],
        out_shape=[jax.ShapeDtypeStruct((bsz, 2) + sh, F32) for sh in shapes],
        scratch_shapes=[pltpu.VMEM(sh, F32) for sh in shapes],
        compiler_params=_cparams(("arbitrary",) * 3),
        name="mlstm_ctx",
    )(qkv, gates, bi, bf)


def _mlstm_lat(qkv, gates, wide, bi, bf, mh_g, c0, n0, m0, dh):
    bsz, s, w3 = qkv.shape
    dm = w3 // 3
    nc = s // CHUNK
    chunk = lambda p, t: t + p * (nc - 1 - 2 * t)
    ochunk = lambda p, t: nc - 1 - p * t
    kern = functools.partial(_mlstm_kernel, nc=nc, dh=dh, dm=dm, emit_h=True)
    st = lambda shape: pl.BlockSpec((1, 1) + shape, lambda b, p, t: (b, p) + (0,) * len(shape))
    shapes = _mlstm_state_shapes(dh)
    return pl.pallas_call(
        kern,
        grid=(bsz, 2, nc),
        in_specs=[pl.BlockSpec((1, CHUNK, w3), lambda b, p, t: (b, chunk(p, t), 0)),
                  pl.BlockSpec((1, CHUNK, 2 * LANES), lambda b, p, t: (b, chunk(p, t), 0)),
                  pl.BlockSpec((1, CHUNK, dm), lambda b, p, t: (b, ochunk(p, t), 0)),
                  pl.BlockSpec((1, LANES), lambda b, p, t: (0, 0)),
                  pl.BlockSpec((1, LANES), lambda b, p, t: (0, 0)),
                  pl.BlockSpec((1, dm), lambda b, p, t: (0, 0))]
        + [st(sh) for sh in shapes],
        out_specs=pl.BlockSpec((1, CHUNK, dm), lambda b, p, t: (b, ochunk(p, t), 0)),
        out_shape=jax.ShapeDtypeStruct((bsz, s, dm), BF16),
        scratch_shapes=[pltpu.VMEM(sh, F32) for sh in shapes]
        + [pltpu.VMEM((nc, CHUNK, dm), F32)],
        compiler_params=_cparams(("arbitrary",) * 3),
        name="mlstm_lat",
    )(qkv, gates, wide, bi, bf, mh_g.reshape(1, dm), c0, n0, m0)


_LRU_TILE = 512
_CONV_TAPS_BEFORE = 2


def _gelu_tanh(x):
    return x * (0.5 * (1.0 + jnp.tanh(0.7978845608028654 * (x + 0.044715 * (x * x * x)))))


def _block_scan(a, u, carry, row8, rev):
    for d in (1, 2, 4):
        sh = SUBLANES - d if rev else d
        m = (row8 < SUBLANES - d) if rev else (row8 >= d)
        u = a * jnp.where(m, pltpu.roll(u, sh, 0), 0.0) + u
        a = a * jnp.where(m, pltpu.roll(a, sh, 0), 1.0)
    sh = SUBLANES - 1 if rev else 1
    m = (row8 < SUBLANES - 1) if rev else (row8 >= 1)
    h_after = u + a * carry
    h_before = jnp.where(m, pltpu.roll(h_after, sh, 0), carry)
    last = h_after[0:1, :] if rev else h_after[SUBLANES - 1:SUBLANES, :]
    return h_before, h_after, last


def _rglru_kernel(xr_ref, gr_ref, xctx_ref, cw_ref, cb_ref, w4_ref, b4_ref, lam_ref, y_ref,
                  xp_ref, af_ref, uf_ref, ab_ref, ub_ref, *, rows, cols, t_ctx):
    blk = LANES
    s = rows * cols
    cw = cw_ref[...]
    cb = cb_ref[...]
    w4 = w4_ref[0]
    b4 = b4_ref[0]
    sp = _softplus(-lam_ref[...])
    row8 = lax.broadcasted_iota(jnp.int32, (SUBLANES, blk), 0)
    before = _CONV_TAPS_BEFORE

    def gates_to_scratch(xc, r0, n):
        def a_u(r_pre, i_pre, sp_row):
            log_a = (-LRU_C * _sigmoid_tanh(r_pre)) * sp_row
            a = jnp.exp(log_a)
            one_minus_a2 = -jnp.tanh(log_a) * (a * a + 1.0)
            return a, jnp.sqrt(one_minus_a2) * (_sigmoid_tanh(i_pre) * xc)
        z = _dot(xc.astype(BF16), w4) + b4
        a, u = a_u(z[:, 0:blk], z[:, blk:2 * blk], sp[0:1])
        af_ref[pl.ds(r0, n), :] = a
        uf_ref[pl.ds(r0, n), :] = u
        a, u = a_u(z[:, 2 * blk:3 * blk], z[:, 3 * blk:4 * blk], sp[1:2])
        ab_ref[pl.ds(r0, n), :] = a
        ub_ref[pl.ds(r0, n), :] = u

    pad = SUBLANES
    zeros = jnp.zeros((pad, blk), F32)
    xp_ref[0:pad, :] = zeros
    xp_ref[pad:pad + t_ctx, :] = xctx_ref[0]
    xp_ref[pad + t_ctx:2 * pad + t_ctx, :] = zeros
    ext = xp_ref[0:t_ctx + 2 * pad, :]
    xc = cb
    for j in range(cw.shape[0]):
        xc = xc + cw[j:j + 1] * ext[pad - before + j:pad - before + j + t_ctx]
    gates_to_scratch(xc, 0, t_ctx)
    nb = t_ctx // SUBLANES

    def ctx_body(i, carry):
        cf, cbw = carry
        rf = pl.multiple_of(i * SUBLANES, SUBLANES)
        rb = pl.multiple_of((nb - 1 - i) * SUBLANES, SUBLANES)
        _, _, cf = _block_scan(af_ref[pl.ds(rf, SUBLANES), :], uf_ref[pl.ds(rf, SUBLANES), :],
                               cf, row8, False)
        _, _, cbw = _block_scan(ab_ref[pl.ds(rb, SUBLANES), :], ub_ref[pl.ds(rb, SUBLANES), :],
                                cbw, row8, True)
        return cf, cbw
    zero = jnp.zeros((1, blk), F32)
    h0f, h0b = lax.fori_loop(0, nb, ctx_body, (zero, zero), unroll=4)

    rowv = lax.broadcasted_iota(jnp.int32, (rows, blk), 0)

    def copy(i, c):
        r0 = pl.multiple_of(i * _LRU_TILE, _LRU_TILE)
        xp_ref[pl.ds(before * rows + r0, _LRU_TILE), :] = xr_ref[0, pl.ds(r0, _LRU_TILE), :]
        return c
    lax.fori_loop(0, s // _LRU_TILE, copy, 0)
    for k in range(before):
        src = (cols - before + k) * rows
        ext = xr_ref[0, src - SUBLANES:src + rows, :]
        xp_ref[k * rows:(k + 1) * rows, :] = jnp.where(rowv == 0, 0.0,
                                                       ext[SUBLANES - 1:SUBLANES - 1 + rows])
    ext = xr_ref[0, 0:rows + SUBLANES, :]
    xp_ref[(cols + before) * rows:(cols + before + 1) * rows, :] = jnp.where(
        rowv == rows - 1, 0.0, ext[1:1 + rows])

    def fill(i, c):
        r0 = pl.multiple_of(i * _LRU_TILE, _LRU_TILE)
        xc = cb
        for j in range(cw.shape[0]):
            xc = xc + cw[j:j + 1] * xp_ref[pl.ds(r0 + j * rows, _LRU_TILE), :]
        gates_to_scratch(xc, r0, _LRU_TILE)
        return c
    lax.fori_loop(0, s // _LRU_TILE, fill, 0)

    def pass1(i, carry):
        hf, pf, hb, pb = carry
        rf = pl.multiple_of(i * rows, rows)
        rb = pl.multiple_of((cols - 1 - i) * rows, rows)
        a = af_ref[pl.ds(rf, rows), :]
        hf = a * hf + uf_ref[pl.ds(rf, rows), :]
        pf = pf * a
        uf_ref[pl.ds(rf, rows), :] = hf
        af_ref[pl.ds(rf, rows), :] = pf
        a = ab_ref[pl.ds(rb, rows), :]
        hb = a * hb + ub_ref[pl.ds(rb, rows), :]
        pb = pb * a
        ub_ref[pl.ds(rb, rows), :] = hb
        ab_ref[pl.ds(rb, rows), :] = pb
        return hf, pf, hb, pb
    z64 = jnp.zeros((rows, blk), F32)
    o64 = jnp.ones((rows, blk), F32)
    hf_end, pf_end, hb_end, pb_end = lax.fori_loop(0, cols, pass1, (z64, o64, z64, o64), unroll=2)

    nrb = rows // SUBLANES
    carry = h0f
    hin_f = []
    for k in range(nrb):
        sl = slice(k * SUBLANES, (k + 1) * SUBLANES)
        before_k, _, carry = _block_scan(pf_end[sl], hf_end[sl], carry, row8, False)
        hin_f.append(before_k)
    carry = h0b
    hin_b = [None] * nrb
    for k in reversed(range(nrb)):
        sl = slice(k * SUBLANES, (k + 1) * SUBLANES)
        before_k, _, carry = _block_scan(pb_end[sl], hb_end[sl], carry, row8, True)
        hin_b[k] = before_k
    hin_f = jnp.concatenate(hin_f, axis=0)
    hin_b = jnp.concatenate(hin_b, axis=0)

    def emit(i, c):
        r0 = pl.multiple_of(i * rows, rows)
        hsum = (uf_ref[pl.ds(r0, rows), :] + af_ref[pl.ds(r0, rows), :] * hin_f
                + ub_ref[pl.ds(r0, rows), :] + ab_ref[pl.ds(r0, rows), :] * hin_b)
        y_ref[0, pl.ds(r0, rows), :] = (_gelu_tanh(gr_ref[0, pl.ds(r0, rows), :]) * hsum).astype(BF16)
        return c
    lax.fori_loop(0, cols, emit, 0, unroll=2)


def _rglru(wide, wide_ctx, conv_w, conv_b, w4, b4, lam, dm, rows, cols):
    bsz, s, _ = wide.shape
    t_ctx = wide_ctx.shape[1]
    nblk = dm // LANES
    assert conv_w.shape[0] == _CONV_TAPS_BEFORE + 2 and s == rows * cols
    kern = functools.partial(_rglru_kernel, rows=rows, cols=cols, t_ctx=t_ctx)
    col = lambda shape, off: pl.BlockSpec(shape, lambda b, n: (b, 0, off + n))
    par = lambda shape: pl.BlockSpec(shape, lambda b, n: (0,) * (len(shape) - 1) + (n,))
    return pl.pallas_call(
        kern,
        grid=(bsz, nblk),
        in_specs=[col((1, s, LANES), nblk), col((1, s, LANES), 2 * nblk),
                  col((1, t_ctx, LANES), nblk),
                  par((conv_w.shape[0], LANES)), par((1, LANES)),
                  pl.BlockSpec((1, LANES, 4 * LANES), lambda b, n: (n, 0, 0)),
                  pl.BlockSpec((1, 1, 4 * LANES), lambda b, n: (n, 0, 0)),
                  par((2, LANES))],
        out_specs=pl.BlockSpec((1, s, LANES), lambda b, n: (b, 0, n)),
        out_shape=jax.ShapeDtypeStruct((bsz, s, dm), BF16),
        scratch_shapes=[pltpu.VMEM((s + (conv_w.shape[0] - 1) * rows, LANES), F32)]
        + [pltpu.VMEM((s, LANES), F32)] * 4,
        compiler_params=_cparams(("arbitrary", "arbitrary")),
        name="rglru",
    )(wide, wide, wide_ctx, conv_w, conv_b.reshape(1, dm), w4, b4, lam)


def _outproj_kernel(ym_ref, yr_ref, w_ref, x_ref, gpost_ref, gpre_ref,
                    gate_ref, scale_ref, shift_ref, x1_ref, h2_ref, *, dm):
    y = _dot(ym_ref[...], w_ref[0:dm, :]) + _dot(yr_ref[...], w_ref[dm:2 * dm, :])
    x1_ref[...] = _grid_to_colmajor(x_ref[0])
    x1 = x1_ref[...] + gate_ref[0] * _rms(y, gpost_ref[...])
    x1_ref[...] = x1
    h2 = _rms(x1, gpre_ref[...]) * (1.0 + scale_ref[0]) + shift_ref[0]
    h2_ref[...] = h2.astype(BF16)


def _outproj(ym, yr, w_out, x4, g_post, g_pre, mod3):
    bsz, rows, cols, d = x4.shape
    n, dm = ym.shape
    tm = rows * COLS_PER_TILE
    tps = cols // COLS_PER_TILE
    kern = functools.partial(_outproj_kernel, dm=dm)
    modspec = lambda k: pl.BlockSpec((1, 1, d), lambda i: (i // tps, 0, k))
    vec = pl.BlockSpec((1, d), lambda i: (0, 0))
    return pl.pallas_call(
        kern,
        grid=(n // tm,),
        in_specs=[pl.BlockSpec((tm, dm), lambda i: (i, 0)),
                  pl.BlockSpec((tm, dm), lambda i: (i, 0)),
                  pl.BlockSpec((2 * dm, d), lambda i: (0, 0)),
                  pl.BlockSpec((1, rows, COLS_PER_TILE, d), lambda i: (i // tps, 0, i % tps, 0)),
                  vec, vec, modspec(2), modspec(4), modspec(3)],
        out_specs=[pl.BlockSpec((tm, d), lambda i: (i, 0)),
                   pl.BlockSpec((tm, d), lambda i: (i, 0))],
        out_shape=[jax.ShapeDtypeStruct((n, d), F32), jax.ShapeDtypeStruct((n, d), BF16)],
        compiler_params=_cparams(("arbitrary",)),
        name="outproj",
    )(ym, yr, w_out, x4, g_post.reshape(1, d), g_pre.reshape(1, d), mod3, mod3, mod3)


def _ffn_in_kernel(h_ref, wg_ref, wu_ref, o_ref):
    h = h_ref[...]
    g = _dot(h, wg_ref[...])
    u = _dot(h, wu_ref[...])
    o_ref[...] = ((g * _sigmoid(g)) * u).astype(BF16)


def _ffn_in(h2, w_in, d_ff, tm, tn):
    n, d = h2.shape
    nj = d_ff // tn
    return pl.pallas_call(
        _ffn_in_kernel,
        grid=(n // tm, nj),
        in_specs=[pl.BlockSpec((tm, d), lambda i, j: (i, 0)),
                  pl.BlockSpec((d, tn), lambda i, j: (0, j)),
                  pl.BlockSpec((d, tn), lambda i, j: (0, j + nj))],
        out_specs=pl.BlockSpec((tm, tn), lambda i, j: (i, j)),
        out_shape=jax.ShapeDtypeStruct((n, d_ff), BF16),
        compiler_params=_cparams(("arbitrary", "arbitrary")),
        name="ffn_in",
    )(h2, w_in, w_in)


def _ffn_out_kernel(a_ref, w_ref, x1_ref, g_ref, gate_ref, o_ref, acc_ref):
    k = pl.program_id(1)
    nk = pl.num_programs(1)

    @pl.when(k == 0)
    def _():
        acc_ref[...] = _dot(a_ref[...], w_ref[...])

    @pl.when(jnp.logical_and(k > 0, k < nk - 1))
    def _():
        acc_ref[...] += _dot(a_ref[...], w_ref[...])

    @pl.when(k == nk - 1)
    def _():
        f = acc_ref[...] + _dot(a_ref[...], w_ref[...])
        acc_ref[...] = x1_ref[...] + gate_ref[0] * _rms(f, g_ref[...])
        o_ref[0] = _colmajor_to_grid(acc_ref[...], o_ref.shape[1])


def _ffn_out(act, w_out, x1, g_post, mod3, bsz, rows, cols, n_k):
    n, d_ff = act.shape
    d = x1.shape[1]
    tm = rows * COLS_PER_TILE
    tps = cols // COLS_PER_TILE
    tk = d_ff // n_k
    assert n_k >= 2 and tk % LANES == 0
    return pl.pallas_call(
        _ffn_out_kernel,
        grid=(n // tm, n_k),
        in_specs=[pl.BlockSpec((tm, tk), lambda i, k: (i, k)),
                  pl.BlockSpec((tk, d), lambda i, k: (k, 0)),
                  pl.BlockSpec((tm, d), lambda i, k: (i, 0)),
                  pl.BlockSpec((1, d), lambda i, k: (0, 0)),
                  pl.BlockSpec((1, 1, d), lambda i, k: (i // tps, 0, 5))],
        out_specs=pl.BlockSpec((1, rows, COLS_PER_TILE, d),
                               lambda i, k: (i // tps, 0, i % tps, 0)),
        out_shape=jax.ShapeDtypeStruct((bsz, rows, cols, d), F32),
        scratch_shapes=[pltpu.VMEM((tm, d), F32)],
        compiler_params=_cparams(("arbitrary", "arbitrary")),
        name="ffn_out",
    )(act, w_out, x1, g_post.reshape(1, d), mod3)


def kernel(x, c, ctx, c_ctx, w_mod, b_mod, g_pre_mix, g_post_mix, g_pre_ffn, g_post_ffn,
           w_in, b_gates, mh_norm_g, conv_w, conv_b, lru_w_a, lru_b_a, lru_w_x, lru_b_x,
           lru_lambda, w_out, w_ffn_in, w_ffn_out):
    bsz, s, d = x.shape
    t_ctx = ctx.shape[1]
    depth = w_mod.shape[0]
    assert depth == 1, "context-stream update between layers is not implemented"
    cols = GRID_W
    rows = s // cols
    dm = mh_norm_g.shape[1]
    assert dm == conv_w.shape[2]
    dh = dm // N_HEADS
    n_gates = N_GATE_TYPES * N_HEADS
    d_ff = w_ffn_out.shape[1]
    layer = 0

    wi = w_in[layer]
    o4 = 4 * dm
    o5 = o4 + n_gates
    w_big = jnp.concatenate([wi[:, :o4], wi[:, o5:]], axis=1).astype(BF16)
    wgt = wi[:, o4:o5].reshape(d, N_GATE_TYPES, N_HEADS)
    zpad = jnp.zeros((d, LANES - 2 * N_HEADS), wi.dtype)
    w_gate = jnp.concatenate([wgt[:, 0], wgt[:, 2], zpad, wgt[:, 1], wgt[:, 3], zpad],
                             axis=1).astype(BF16)
    bg = b_gates[layer].astype(F32)
    bpad = jnp.zeros((LANES - 2 * N_HEADS,), F32)
    bi = jnp.concatenate([bg[0], bg[2], bpad]).reshape(1, LANES)
    bf = jnp.concatenate([bg[1], bg[3], bpad]).reshape(1, LANES)
    nblk = lru_w_a.shape[2]
    wa, wx = lru_w_a[layer], lru_w_x[layer]
    w4 = jnp.concatenate([wa[0], wx[0], wa[1], wx[1]], axis=2).astype(BF16)
    ba = lru_b_a[layer].reshape(2, nblk, 1, LANES)
    bx = lru_b_x[layer].reshape(2, nblk, 1, LANES)
    b4 = jnp.concatenate([ba[0], bx[0], ba[1], bx[1]], axis=2)
    w_o = w_out[layer].astype(BF16)
    w_f1 = w_ffn_in[layer].astype(BF16)
    w_f2 = w_ffn_out[layer].astype(BF16)

    n_rows = SUBLANES * ((bsz + 1 + SUBLANES - 1) // SUBLANES)
    c_rows = jnp.concatenate([c, c_ctx[None], jnp.zeros((n_rows - bsz - 1, d), c.dtype)], axis=0)
    mod3 = _modulation(c_rows, w_mod[layer], b_mod[layer]).reshape(n_rows, 1, 6 * d)

    x4 = x.reshape(bsz, rows, cols, d)
    tm = rows * COLS_PER_TILE
    tps = cols // COLS_PER_TILE
    qkv, wide, gates = _inproj(x4, g_pre_mix[layer], mod3, lambda i: i // tps,
                               w_big, w_gate, 3 * dm, tm, 1536)
    qkv_c, wide_c, gates_c = _inproj(ctx.reshape(bsz * t_ctx, d), g_pre_mix[layer], mod3,
                                     lambda i: bsz, w_big, w_gate, 3 * dm, tm, 1536)
    wide = wide.reshape(bsz, s, 3 * dm)
    wide_c = wide_c.reshape(bsz, t_ctx, 3 * dm)

    c0, n0, m0 = _mlstm_ctx(qkv_c.reshape(bsz, t_ctx, 3 * dm),
                            gates_c.reshape(bsz, t_ctx, 2 * LANES), bi, bf, dh)
    y_m = _mlstm_lat(qkv.reshape(bsz, s, 3 * dm), gates.reshape(bsz, s, 2 * LANES), wide,
                     bi, bf, mh_norm_g[layer], c0, n0, m0, dh)

    y_r = _rglru(wide, wide_c, conv_w[layer], conv_b[layer], w4, b4, lru_lambda[layer],
                 dm, rows, cols)

    x1, h2 = _outproj(y_m.reshape(bsz * s, dm), y_r.reshape(bsz * s, dm), w_o, x4,
                      g_post_mix[layer], g_pre_ffn[layer], mod3)
    act = _ffn_in(h2, w_f1, d_ff, 1024, 512)
    out = _ffn_out(act, w_f2, x1, g_post_ffn[layer], mod3, bsz, rows, cols, 2)
    return out.reshape(bsz, s, d)
```

```python
import functools

import jax
import jax.numpy as jnp
from jax import lax
from jax.experimental import pallas as pl
from jax.experimental.pallas import tpu as pltpu

F32 = jnp.float32
BF16 = jnp.bfloat16

GRID_W = 64
N_HEADS = 4
CHUNK = 256
LRU_C = 8.0
EPS = 1e-6
N_GATE_TYPES = 4
GATE_LANES = 2 * N_HEADS

LANES = 128
SUBLANES = 8
VMEM_LIMIT_BYTES = 56 * 1024 * 1024

COLS_PER_TILE = SUBLANES


def _cparams(sem):
    return pltpu.CompilerParams(dimension_semantics=sem,
                                vmem_limit_bytes=VMEM_LIMIT_BYTES)


def _rms(xf, g):
    return xf * lax.rsqrt(jnp.mean(xf * xf, axis=-1, keepdims=True) + EPS) * g


def _sigmoid(x):
    return 1.0 / (1.0 + jnp.exp(-x))


def _sigmoid_tanh(x):
    return 0.5 * jnp.tanh(0.5 * x) + 0.5


def _softplus(x):
    return jnp.maximum(x, 0.0) + jnp.log1p(jnp.exp(-jnp.abs(x)))


def _dot(a, b):
    return jnp.dot(a, b, preferred_element_type=F32)


def _mod_kernel(c_ref, w_ref, b_ref, o_ref):
    cv = c_ref[...]
    o_ref[...] = _dot(cv * _sigmoid(cv), w_ref[...]) + b_ref[...]


def _modulation(c_rows, w_mod, b_mod, tn=512):
    r, d = c_rows.shape
    n = w_mod.shape[1]
    return pl.pallas_call(
        _mod_kernel,
        grid=(n // tn,),
        in_specs=[pl.BlockSpec((r, d), lambda j: (0, 0)),
                  pl.BlockSpec((d, tn), lambda j: (0, j)),
                  pl.BlockSpec((1, tn), lambda j: (0, j))],
        out_specs=pl.BlockSpec((r, tn), lambda j: (0, j)),
        out_shape=jax.ShapeDtypeStruct((r, n), F32),
        compiler_params=_cparams(("arbitrary",)),
        name="mod",
    )(c_rows, w_mod, b_mod.reshape(1, n))


def _grid_to_colmajor(x_blk):
    rows, ct, d = x_blk.shape
    return jnp.swapaxes(x_blk, 0, 1).reshape(ct * rows, d)


def _colmajor_to_grid(y, rows):
    n, d = y.shape
    return jnp.swapaxes(y.reshape(n // rows, rows, d), 0, 1)


def _inproj_kernel(x_ref, g_ref, shift_ref, scale_ref, w_ref, wg_ref,
                   qkv_ref, wide_ref, gate_ref, hx_ref, *scratch, n_bf16_tiles, col_tile):
    j = pl.program_id(1)

    def norm_mod(xf):
        return (_rms(xf, g_ref[...]) * (1.0 + scale_ref[0]) + shift_ref[0]).astype(BF16)

    @pl.when(j == 0)
    def _():
        if col_tile:
            xt_ref, = scratch
            xt_ref[...] = _grid_to_colmajor(x_ref[0])
            hx_ref[...] = norm_mod(xt_ref[...])
        else:
            hx_ref[...] = norm_mod(x_ref[...])
        gate_ref[...] = _dot(hx_ref[...], wg_ref[...])

    @pl.when(j < n_bf16_tiles)
    def _():
        qkv_ref[...] = _dot(hx_ref[...], w_ref[...]).astype(BF16)

    @pl.when(j >= n_bf16_tiles)
    def _():
        wide_ref[...] = _dot(hx_ref[...], w_ref[...])


def _inproj(x, g, mod3, mod_row, w_big, w_gate, n_bf16, tm, tn):
    d = x.shape[-1]
    col_tile = x.ndim == 4
    if col_tile:
        bsz, rows, cols, _ = x.shape
        n = bsz * rows * cols
        ct = tm // rows
        tiles_per_sample = cols // ct
        x_spec = pl.BlockSpec((1, rows, ct, d),
                              lambda i, j: (i // tiles_per_sample, 0, i % tiles_per_sample, 0))
    else:
        n = x.shape[0]
        x_spec = pl.BlockSpec((tm, d), lambda i, j: (i, 0))
    n_tot = w_big.shape[1]
    n_bt = n_bf16 // tn
    n_wt = (n_tot - n_bf16) // tn
    kern = functools.partial(_inproj_kernel, n_bf16_tiles=n_bt, col_tile=col_tile)
    return pl.pallas_call(
        kern,
        grid=(n // tm, n_bt + n_wt),
        in_specs=[x_spec,
                  pl.BlockSpec((1, d), lambda i, j: (0, 0)),
                  pl.BlockSpec((1, 1, d), lambda i, j: (mod_row(i), 0, 0)),
                  pl.BlockSpec((1, 1, d), lambda i, j: (mod_row(i), 0, 1)),
                  pl.BlockSpec((d, tn), lambda i, j: (0, j)),
                  pl.BlockSpec((d, 2 * LANES), lambda i, j: (0, 0))],
        out_specs=[pl.BlockSpec((tm, tn), lambda i, j: (i, jnp.minimum(j, n_bt - 1))),
                   pl.BlockSpec((tm, tn), lambda i, j: (i, jnp.maximum(j - n_bt, 0))),
                   pl.BlockSpec((tm, 2 * LANES), lambda i, j: (i, 0))],
        out_shape=[jax.ShapeDtypeStruct((n, n_bf16), BF16),
                   jax.ShapeDtypeStruct((n, n_tot - n_bf16), F32),
                   jax.ShapeDtypeStruct((n, 2 * LANES), F32)],
        scratch_shapes=[pltpu.VMEM((tm, d), BF16)] + ([pltpu.VMEM((tm, d), F32)] if col_tile else []),
        compiler_params=_cparams(("arbitrary", "arbitrary")),
        name="inproj",
    )(x, g.reshape(1, d), mod3, mod3, w_big, w_gate)


def _keep_mask(n, rev):
    r = lax.broadcasted_iota(jnp.int32, (n, n), 0)
    c = lax.broadcasted_iota(jnp.int32, (n, n), 1)
    return (c >= r) if rev else (c <= r)


def _split3(x):
    hi = x.astype(BF16)
    r1 = x - hi.astype(F32)
    mid = r1.astype(BF16)
    lo = (r1 - mid.astype(F32)).astype(BF16)
    return hi, mid, lo


def _split2(x):
    hi = x.astype(BF16)
    return hi, (x - hi.astype(F32)).astype(BF16)


def _lane_tile(x, width):
    return jnp.concatenate([x] * (width // x.shape[1]), axis=1)


def _gate_prep(gi, gf, bi, bf, rev):
    n = gi.shape[0]
    ig = gi + bi
    xf = gf + bf
    lf = jnp.minimum(xf, 0.0) - jnp.log1p(jnp.exp(-jnp.abs(xf)))
    tri = _keep_mask(n, rev).astype(BF16)
    h, m, l = _split3(lf)
    cum = _dot(tri, h) + _dot(tri, m) + _dot(tri, l)
    lane = lax.broadcasted_iota(jnp.int32, cum.shape, 1)
    z = jnp.where(lane < GATE_LANES, cum, ig - cum)
    return z, z.T


def _mlstm_state_terms(k, v_aug, cb, igcb, m_prev, rev):
    n = k.shape[0]
    tot = cb[0:1, :] if rev else cb[n - 1:n, :]
    w_log = tot + igcb
    m_new = jnp.maximum(tot + m_prev, jnp.max(w_log, axis=0, keepdims=True))
    a = jnp.exp(tot + m_prev - m_new)
    w = jnp.exp(w_log - m_new).astype(BF16)
    wk = k * _lane_tile(w, k.shape[1])
    ct_loc = lax.dot_general(wk, v_aug, (((0,), (0,)), ((), ())), preferred_element_type=F32)
    return a, ct_loc, m_new


def _mlstm_decay_weights(s, cb, igc_r, m_prev, keep):
    n = s.shape[0]
    log_d = jnp.where(keep, _lane_tile(cb, n) + igc_r, -jnp.inf)
    log_inter = cb + m_prev
    m_row = jnp.maximum(log_inter, jnp.max(log_d, axis=1, keepdims=True))
    dmat = jnp.exp(log_d - _lane_tile(m_row, n))
    return (s * dmat).astype(BF16), jnp.exp(log_inter - m_row).astype(BF16), jnp.exp(-m_row)


def _mlstm_kernel(*refs, nc, dh, dm, emit_h):
    if emit_h:
        (qkv_ref, g_ref, o_ref, bi_ref, bf_ref, sel_ref, mhg_ref, c0_ref, m0_ref,
         y_ref, ct_ref, m_ref, hf_ref) = refs
    else:
        (qkv_ref, g_ref, bi_ref, bf_ref, sel_ref, c_out, m_out, ct_ref, m_ref) = refs
    p = pl.program_id(1)
    t = pl.program_id(2)

    @pl.when(t == 0)
    def _():
        if emit_h:
            ct_ref[...] = c0_ref[0, 0]
            m_ref[...] = m0_ref[0, 0]
        else:
            ct_ref[...] = jnp.zeros_like(ct_ref)
            m_ref[...] = jnp.zeros_like(m_ref)

    def run(rev):
        z, z_t = _gate_prep(g_ref[0, :, 0:LANES], g_ref[0, :, LANES:2 * LANES],
                            bi_ref[...], bf_ref[...], rev)
        zh, zl = _split2(z)
        keep = _keep_mask(CHUNK, rev)
        ones = jnp.ones((CHUNK, LANES), BF16)
        heads = range(N_HEADS)
        lanes = [h + N_HEADS * int(rev) for h in heads]
        hsl = [slice(h * dh, (h + 1) * dh) for h in heads]
        q = [qkv_ref[0, :, hsl[h]] for h in heads]
        k = [qkv_ref[0, :, dm + h * dh:dm + (h + 1) * dh] for h in heads]
        v_aug = [jnp.concatenate([qkv_ref[0, :, 2 * dm + h * dh:2 * dm + (h + 1) * dh], ones],
                                 axis=1) for h in heads]
        m_prev = [m_ref[h, 0:1, :] for h in heads]
        if emit_h:
            s = [lax.dot_general(q[h], k[h], (((1,), (1,)), ((), ())),
                                 preferred_element_type=F32) for h in heads]
        cols = [_dot(zh, sel_ref[l]) + _dot(zl, sel_ref[l]) for l in lanes]
        cb = [c[:, :LANES] for c in cols]
        igcb = [c[:, LANES:] for c in cols]
        upd = [_mlstm_state_terms(k[h], v_aug[h], cb[h], igcb[h], m_prev[h], rev) for h in heads]
        if emit_h:
            wts = [_mlstm_decay_weights(s[h], cb[h], z_t[GATE_LANES + lanes[h]:GATE_LANES + lanes[h] + 1, :],
                                        m_prev[h], keep) for h in heads]
            num = [_dot(wts[h][0], v_aug[h])
                   + _dot(q[h] * _lane_tile(wts[h][1], dh), ct_ref[h].astype(BF16)) for h in heads]
            for h in heads:
                r = 1.0 / jnp.maximum(jnp.abs(num[h][:, dh:]), wts[h][2])
                hout = num[h][:, :dh] * _lane_tile(r, dh)
                if not rev:
                    hf_ref[t, :, hsl[h]] = hout
                else:
                    hsum = hout + hf_ref[nc - 1 - t, :, hsl[h]]
                    hn = hsum * lax.rsqrt(jnp.mean(hsum * hsum, axis=1, keepdims=True) + EPS)
                    y_ref[0, :, hsl[h]] = (_sigmoid(o_ref[0, :, hsl[h]])
                                           * (hn * mhg_ref[:, hsl[h]])).astype(BF16)
        for h in heads:
            a, ct_loc, m_new = upd[h]
            ct_ref[h] = _lane_tile(a, dh + LANES) * ct_ref[h] + ct_loc
            m_ref[h] = jnp.broadcast_to(m_new, m_ref.shape[1:])

    @pl.when(p == 0)
    def _():
        run(False)

    @pl.when(p == 1)
    def _():
        run(True)

    if not emit_h:
        @pl.when(t == nc - 1)
        def _():
            c_out[0, 0] = ct_ref[...]
            m_out[0, 0] = m_ref[...]


def _mlstm_state_shapes(dh):
    return [(N_HEADS, dh, dh + LANES), (N_HEADS, SUBLANES, LANES)]


def _mlstm_select():
    row = lax.broadcasted_iota(jnp.int32, (GATE_LANES, LANES, 2 * LANES), 1)
    col = lax.broadcasted_iota(jnp.int32, (GATE_LANES, LANES, 2 * LANES), 2)
    lane = lax.broadcasted_iota(jnp.int32, (GATE_LANES, LANES, 2 * LANES), 0)
    return (row == jnp.where(col < LANES, lane, lane + GATE_LANES)).astype(BF16)


def _mlstm_ctx(qkv, gates, bi, bf, sel, dh):
    bsz, t, w3 = qkv.shape
    dm = w3 // 3
    nc = t // CHUNK
    chunk = lambda p, s: s + p * (nc - 1 - 2 * s)
    kern = functools.partial(_mlstm_kernel, nc=nc, dh=dh, dm=dm, emit_h=False)
    st = lambda shape: pl.BlockSpec((1, 1) + shape, lambda b, p, s: (b, p) + (0,) * len(shape))
    shapes = _mlstm_state_shapes(dh)
    return pl.pallas_call(
        kern,
        grid=(bsz, 2, nc),
        in_specs=[pl.BlockSpec((1, CHUNK, w3), lambda b, p, s: (b, chunk(p, s), 0)),
                  pl.BlockSpec((1, CHUNK, 2 * LANES), lambda b, p, s: (b, chunk(p, s), 0)),
                  pl.BlockSpec((1, LANES), lambda b, p, s: (0, 0)),
                  pl.BlockSpec((1, LANES), lambda b, p, s: (0, 0)),
                  pl.BlockSpec(sel.shape, lambda b, p, s: (0, 0, 0))],
        out_specs=[st(sh) for sh in shapes],
        out_shape=[jax.ShapeDtypeStruct((bsz, 2) + sh, F32) for sh in shapes],
        scratch_shapes=[pltpu.VMEM(sh, F32) for sh in shapes],
        compiler_params=_cparams(("arbitrary",) * 3),
        name="mlstm_ctx",
    )(qkv, gates, bi, bf, sel)


def _mlstm_lat(qkv, gates, wide, bi, bf, sel, mh_g, c0, m0, dh):
    bsz, s, w3 = qkv.shape
    dm = w3 // 3
    nc = s // CHUNK
    chunk = lambda p, t: t + p * (nc - 1 - 2 * t)
    ochunk = lambda p, t: nc - 1 - p * t
    kern = functools.partial(_mlstm_kernel, nc=nc, dh=dh, dm=dm, emit_h=True)
    st = lambda shape: pl.BlockSpec((1, 1) + shape, lambda b, p, t: (b, p) + (0,) * len(shape))
    shapes = _mlstm_state_shapes(dh)
    return pl.pallas_call(
        kern,
        grid=(bsz, 2, nc),
        in_specs=[pl.BlockSpec((1, CHUNK, w3), lambda b, p, t: (b, chunk(p, t), 0)),
                  pl.BlockSpec((1, CHUNK, 2 * LANES), lambda b, p, t: (b, chunk(p, t), 0)),
                  pl.BlockSpec((1, CHUNK, dm), lambda b, p, t: (b, ochunk(p, t), 0)),
                  pl.BlockSpec((1, LANES), lambda b, p, t: (0, 0)),
                  pl.BlockSpec((1, LANES), lambda b, p, t: (0, 0)),
                  pl.BlockSpec(sel.shape, lambda b, p, t: (0, 0, 0)),
                  pl.BlockSpec((1, dm), lambda b, p, t: (0, 0))]
        + [st(sh) for sh in shapes],
        out_specs=pl.BlockSpec((1, CHUNK, dm), lambda b, p, t: (b, ochunk(p, t), 0)),
        out_shape=jax.ShapeDtypeStruct((bsz, s, dm), BF16),
        scratch_shapes=[pltpu.VMEM(sh, F32) for sh in shapes]
        + [pltpu.VMEM((nc, CHUNK, dm), F32)],
        compiler_params=_cparams(("arbitrary",) * 3),
        name="mlstm_lat",
    )(qkv, gates, wide, bi, bf, sel, mh_g.reshape(1, dm), c0, m0)


_LRU_TILE = 512
_CONV_TAPS_BEFORE = 2
_F32_TINY = float(jnp.finfo(jnp.float32).tiny)


def _gelu_tanh(x):
    return x * (0.5 * (1.0 + jnp.tanh(0.7978845608028654 * (x + 0.044715 * (x * x * x)))))


def _block_scan(a, u, carry, row8, rev):
    for d in (1, 2, 4):
        sh = SUBLANES - d if rev else d
        m = (row8 < SUBLANES - d) if rev else (row8 >= d)
        u = a * jnp.where(m, pltpu.roll(u, sh, 0), 0.0) + u
        a = a * jnp.where(m, pltpu.roll(a, sh, 0), 1.0)
    sh = SUBLANES - 1 if rev else 1
    m = (row8 < SUBLANES - 1) if rev else (row8 >= 1)
    h_after = u + a * carry
    h_before = jnp.where(m, pltpu.roll(h_after, sh, 0), carry)
    last = h_after[0:1, :] if rev else h_after[SUBLANES - 1:SUBLANES, :]
    return h_before, h_after, last


def _rglru_kernel(xr_ref, gr_ref, xctx_ref, cw_ref, cb_ref, w4_ref, b4_ref, lam_ref, y_ref,
                  xp_ref, af_ref, uf_ref, ab_ref, ub_ref, *, rows, cols, t_ctx):
    blk = LANES
    s = rows * cols
    cw = cw_ref[...]
    cb = cb_ref[...]
    w4 = w4_ref[0]
    b4 = b4_ref[0]
    sp = _softplus(-lam_ref[...])
    row8 = lax.broadcasted_iota(jnp.int32, (SUBLANES, blk), 0)
    before = _CONV_TAPS_BEFORE

    half_c_sp = (0.5 * LRU_C) * sp

    def gates_to_scratch(xc, r0, n):
        half_xc = 0.5 * xc

        def a_u(r_half, i_half, c_row):
            tr = jnp.tanh(r_half)
            neg_log_a = c_row * tr + c_row
            a = jnp.exp(-neg_log_a)
            one_minus_a2 = jnp.tanh(neg_log_a) * (a * a + 1.0)
            root = one_minus_a2 * lax.rsqrt(jnp.maximum(one_minus_a2, _F32_TINY))
            ti = jnp.tanh(i_half)
            return a, root * (half_xc * ti + half_xc)
        z = _dot(xc.astype(BF16), w4) + b4
        a, u = a_u(z[:, 0:blk], z[:, blk:2 * blk], half_c_sp[0:1])
        af_ref[pl.ds(r0, n), :] = a
        uf_ref[pl.ds(r0, n), :] = u
        a, u = a_u(z[:, 2 * blk:3 * blk], z[:, 3 * blk:4 * blk], half_c_sp[1:2])
        ab_ref[pl.ds(r0, n), :] = a
        ub_ref[pl.ds(r0, n), :] = u

    pad = SUBLANES
    zeros = jnp.zeros((pad, blk), F32)
    xp_ref[0:pad, :] = zeros
    xp_ref[pad:pad + t_ctx, :] = xctx_ref[0]
    xp_ref[pad + t_ctx:2 * pad + t_ctx, :] = zeros
    ext = xp_ref[0:t_ctx + 2 * pad, :]
    xc = cb
    for j in range(cw.shape[0]):
        xc = xc + cw[j:j + 1] * ext[pad - before + j:pad - before + j + t_ctx]
    gates_to_scratch(xc, 0, t_ctx)
    nb = t_ctx // SUBLANES

    def ctx_body(i, carry):
        cf, cbw = carry
        rf = pl.multiple_of(i * SUBLANES, SUBLANES)
        rb = pl.multiple_of((nb - 1 - i) * SUBLANES, SUBLANES)
        _, _, cf = _block_scan(af_ref[pl.ds(rf, SUBLANES), :], uf_ref[pl.ds(rf, SUBLANES), :],
                               cf, row8, False)
        _, _, cbw = _block_scan(ab_ref[pl.ds(rb, SUBLANES), :], ub_ref[pl.ds(rb, SUBLANES), :],
                                cbw, row8, True)
        return cf, cbw
    zero = jnp.zeros((1, blk), F32)
    h0f, h0b = lax.fori_loop(0, nb, ctx_body, (zero, zero), unroll=4)

    rowv = lax.broadcasted_iota(jnp.int32, (rows, blk), 0)

    def copy(i, c):
        r0 = pl.multiple_of(i * _LRU_TILE, _LRU_TILE)
        xp_ref[pl.ds(before * rows + r0, _LRU_TILE), :] = xr_ref[0, pl.ds(r0, _LRU_TILE), :]
        return c
    lax.fori_loop(0, s // _LRU_TILE, copy, 0)
    for k in range(before):
        src = (cols - before + k) * rows
        ext = xr_ref[0, src - SUBLANES:src + rows, :]
        xp_ref[k * rows:(k + 1) * rows, :] = jnp.where(rowv == 0, 0.0,
                                                       ext[SUBLANES - 1:SUBLANES - 1 + rows])
    ext = xr_ref[0, 0:rows + SUBLANES, :]
    xp_ref[(cols + before) * rows:(cols + before + 1) * rows, :] = jnp.where(
        rowv == rows - 1, 0.0, ext[1:1 + rows])

    def fill(i, c):
        r0 = pl.multiple_of(i * _LRU_TILE, _LRU_TILE)
        xc = cb
        for j in range(cw.shape[0]):
            xc = xc + cw[j:j + 1] * xp_ref[pl.ds(r0 + j * rows, _LRU_TILE), :]
        gates_to_scratch(xc, r0, _LRU_TILE)
        return c
    lax.fori_loop(0, s // _LRU_TILE, fill, 0)

    def pass1(i, carry):
        hf, pf, hb, pb = carry
        rf = pl.multiple_of(i * rows, rows)
        rb = pl.multiple_of((cols - 1 - i) * rows, rows)
        a = af_ref[pl.ds(rf, rows), :]
        hf = a * hf + uf_ref[pl.ds(rf, rows), :]
        pf = pf * a
        uf_ref[pl.ds(rf, rows), :] = hf
        af_ref[pl.ds(rf, rows), :] = pf
        a = ab_ref[pl.ds(rb, rows), :]
        hb = a * hb + ub_ref[pl.ds(rb, rows), :]
        pb = pb * a
        ub_ref[pl.ds(rb, rows), :] = hb
        ab_ref[pl.ds(rb, rows), :] = pb
        return hf, pf, hb, pb
    z64 = jnp.zeros((rows, blk), F32)
    o64 = jnp.ones((rows, blk), F32)
    hf_end, pf_end, hb_end, pb_end = lax.fori_loop(0, cols, pass1, (z64, o64, z64, o64), unroll=2)

    nrb = rows // SUBLANES
    carry = h0f
    hin_f = []
    for k in range(nrb):
        sl = slice(k * SUBLANES, (k + 1) * SUBLANES)
        before_k, _, carry = _block_scan(pf_end[sl], hf_end[sl], carry, row8, False)
        hin_f.append(before_k)
    carry = h0b
    hin_b = [None] * nrb
    for k in reversed(range(nrb)):
        sl = slice(k * SUBLANES, (k + 1) * SUBLANES)
        before_k, _, carry = _block_scan(pb_end[sl], hb_end[sl], carry, row8, True)
        hin_b[k] = before_k
    hin_f = jnp.concatenate(hin_f, axis=0)
    hin_b = jnp.concatenate(hin_b, axis=0)

    def emit(i, c):
        r0 = pl.multiple_of(i * rows, rows)
        hsum = (uf_ref[pl.ds(r0, rows), :] + af_ref[pl.ds(r0, rows), :] * hin_f
                + ub_ref[pl.ds(r0, rows), :] + ab_ref[pl.ds(r0, rows), :] * hin_b)
        y_ref[0, pl.ds(r0, rows), :] = (_gelu_tanh(gr_ref[0, pl.ds(r0, rows), :]) * hsum).astype(BF16)
        return c
    lax.fori_loop(0, cols, emit, 0, unroll=2)


def _rglru(wide, wide_ctx, conv_w, conv_b, w4, b4, lam, dm, rows, cols):
    bsz, s, _ = wide.shape
    t_ctx = wide_ctx.shape[1]
    nblk = dm // LANES
    assert conv_w.shape[0] == _CONV_TAPS_BEFORE + 2 and s == rows * cols
    kern = functools.partial(_rglru_kernel, rows=rows, cols=cols, t_ctx=t_ctx)
    col = lambda shape, off: pl.BlockSpec(shape, lambda b, n: (b, 0, off + n))
    par = lambda shape: pl.BlockSpec(shape, lambda b, n: (0,) * (len(shape) - 1) + (n,))
    return pl.pallas_call(
        kern,
        grid=(bsz, nblk),
        in_specs=[col((1, s, LANES), nblk), col((1, s, LANES), 2 * nblk),
                  col((1, t_ctx, LANES), nblk),
                  par((conv_w.shape[0], LANES)), par((1, LANES)),
                  pl.BlockSpec((1, LANES, 4 * LANES), lambda b, n: (n, 0, 0)),
                  pl.BlockSpec((1, 1, 4 * LANES), lambda b, n: (n, 0, 0)),
                  par((2, LANES))],
        out_specs=pl.BlockSpec((1, s, LANES), lambda b, n: (b, 0, n)),
        out_shape=jax.ShapeDtypeStruct((bsz, s, dm), BF16),
        scratch_shapes=[pltpu.VMEM((s + (conv_w.shape[0] - 1) * rows, LANES), F32)]
        + [pltpu.VMEM((s, LANES), F32)] * 4,
        compiler_params=_cparams(("arbitrary", "arbitrary")),
        name="rglru",
    )(wide, wide, wide_ctx, conv_w, conv_b.reshape(1, dm), w4, b4, lam)


def _outproj_kernel(ym_ref, yr_ref, w_ref, x_ref, gpost_ref, gpre_ref,
                    gate_ref, scale_ref, shift_ref, x1_ref, h2_ref, *, dm):
    y = _dot(ym_ref[...], w_ref[0:dm, :]) + _dot(yr_ref[...], w_ref[dm:2 * dm, :])
    x1_ref[...] = _grid_to_colmajor(x_ref[0])
    x1 = x1_ref[...] + gate_ref[0] * _rms(y, gpost_ref[...])
    x1_ref[...] = x1
    h2 = _rms(x1, gpre_ref[...]) * (1.0 + scale_ref[0]) + shift_ref[0]
    h2_ref[...] = h2.astype(BF16)


def _outproj(ym, yr, w_out, x4, g_post, g_pre, mod3):
    bsz, rows, cols, d = x4.shape
    n, dm = ym.shape
    tm = rows * COLS_PER_TILE
    tps = cols // COLS_PER_TILE
    kern = functools.partial(_outproj_kernel, dm=dm)
    modspec = lambda k: pl.BlockSpec((1, 1, d), lambda i: (i // tps, 0, k))
    vec = pl.BlockSpec((1, d), lambda i: (0, 0))
    return pl.pallas_call(
        kern,
        grid=(n // tm,),
        in_specs=[pl.BlockSpec((tm, dm), lambda i: (i, 0)),
                  pl.BlockSpec((tm, dm), lambda i: (i, 0)),
                  pl.BlockSpec((2 * dm, d), lambda i: (0, 0)),
                  pl.BlockSpec((1, rows, COLS_PER_TILE, d), lambda i: (i // tps, 0, i % tps, 0)),
                  vec, vec, modspec(2), modspec(4), modspec(3)],
        out_specs=[pl.BlockSpec((tm, d), lambda i: (i, 0)),
                   pl.BlockSpec((tm, d), lambda i: (i, 0))],
        out_shape=[jax.ShapeDtypeStruct((n, d), F32), jax.ShapeDtypeStruct((n, d), BF16)],
        compiler_params=_cparams(("arbitrary",)),
        name="outproj",
    )(ym, yr, w_out, x4, g_post.reshape(1, d), g_pre.reshape(1, d), mod3, mod3, mod3)


def _ffn_in_kernel(h_ref, wg_ref, wu_ref, o_ref):
    h = h_ref[...]
    g = _dot(h, wg_ref[...].astype(BF16))
    u = _dot(h, wu_ref[...].astype(BF16))
    o_ref[...] = ((g * _sigmoid(g)) * u).astype(BF16)


def _ffn_in(h2, w_in, d_ff, tm, tn):
    n, d = h2.shape
    nj = d_ff // tn
    return pl.pallas_call(
        _ffn_in_kernel,
        grid=(n // tm, nj),
        in_specs=[pl.BlockSpec((tm, d), lambda i, j: (i, 0)),
                  pl.BlockSpec((d, tn), lambda i, j: (0, j)),
                  pl.BlockSpec((d, tn), lambda i, j: (0, j + nj))],
        out_specs=pl.BlockSpec((tm, tn), lambda i, j: (i, j)),
        out_shape=jax.ShapeDtypeStruct((n, d_ff), BF16),
        compiler_params=_cparams(("arbitrary", "arbitrary")),
        name="ffn_in",
    )(h2, w_in, w_in)


def _ffn_out_kernel(a_ref, w_ref, x1_ref, g_ref, gate_ref, o_ref, acc_ref):
    k = pl.program_id(1)
    nk = pl.num_programs(1)

    @pl.when(k == 0)
    def _():
        acc_ref[...] = _dot(a_ref[...], w_ref[...])

    @pl.when(jnp.logical_and(k > 0, k < nk - 1))
    def _():
        acc_ref[...] += _dot(a_ref[...], w_ref[...])

    @pl.when(k == nk - 1)
    def _():
        f = acc_ref[...] + _dot(a_ref[...], w_ref[...])
        acc_ref[...] = x1_ref[...] + gate_ref[0] * _rms(f, g_ref[...])
        o_ref[0] = _colmajor_to_grid(acc_ref[...], o_ref.shape[1])


def _ffn_out(act, w_out, x1, g_post, mod3, bsz, rows, cols, n_k):
    n, d_ff = act.shape
    d = x1.shape[1]
    tm = rows * COLS_PER_TILE
    tps = cols // COLS_PER_TILE
    tk = d_ff // n_k
    assert n_k >= 2 and tk % LANES == 0
    return pl.pallas_call(
        _ffn_out_kernel,
        grid=(n // tm, n_k),
        in_specs=[pl.BlockSpec((tm, tk), lambda i, k: (i, k)),
                  pl.BlockSpec((tk, d), lambda i, k: (k, 0)),
                  pl.BlockSpec((tm, d), lambda i, k: (i, 0)),
                  pl.BlockSpec((1, d), lambda i, k: (0, 0)),
                  pl.BlockSpec((1, 1, d), lambda i, k: (i // tps, 0, 5))],
        out_specs=pl.BlockSpec((1, rows, COLS_PER_TILE, d),
                               lambda i, k: (i // tps, 0, i % tps, 0)),
        out_shape=jax.ShapeDtypeStruct((bsz, rows, cols, d), F32),
        scratch_shapes=[pltpu.VMEM((tm, d), F32)],
        compiler_params=_cparams(("arbitrary", "arbitrary")),
        name="ffn_out",
    )(act, w_out, x1, g_post.reshape(1, d), mod3)


def kernel(x, c, ctx, c_ctx, w_mod, b_mod, g_pre_mix, g_post_mix, g_pre_ffn, g_post_ffn,
           w_in, b_gates, mh_norm_g, conv_w, conv_b, lru_w_a, lru_b_a, lru_w_x, lru_b_x,
           lru_lambda, w_out, w_ffn_in, w_ffn_out):
    bsz, s, d = x.shape
    t_ctx = ctx.shape[1]
    depth = w_mod.shape[0]
    assert depth == 1, "context-stream update between layers is not implemented"
    cols = GRID_W
    rows = s // cols
    dm = mh_norm_g.shape[1]
    assert dm == conv_w.shape[2]
    dh = dm // N_HEADS
    n_gates = N_GATE_TYPES * N_HEADS
    d_ff = w_ffn_out.shape[1]
    layer = 0

    wi = w_in[layer]
    o4 = 4 * dm
    o5 = o4 + n_gates
    w_big = jnp.concatenate([wi[:, :dm], wi[:, dm:2 * dm] * (dh ** -0.5), wi[:, 2 * dm:o4],
                             wi[:, o5:]], axis=1).astype(BF16)
    wgt = wi[:, o4:o5].reshape(d, N_GATE_TYPES, N_HEADS)
    w_ig = jnp.concatenate([wgt[:, 0], wgt[:, 2]], axis=1)
    w_fg = jnp.concatenate([wgt[:, 1], wgt[:, 3]], axis=1)
    zpad = lambda n: jnp.zeros((d, n), wi.dtype)
    w_gate = jnp.concatenate([zpad(GATE_LANES), w_ig, zpad(LANES - 2 * GATE_LANES),
                              w_fg, w_fg, zpad(LANES - 2 * GATE_LANES)], axis=1).astype(BF16)
    bg = b_gates[layer].astype(F32)
    b_ig = jnp.concatenate([bg[0], bg[2]])
    b_fg = jnp.concatenate([bg[1], bg[3]])
    bpad = lambda n: jnp.zeros((n,), F32)
    bi = jnp.concatenate([bpad(GATE_LANES), b_ig, bpad(LANES - 2 * GATE_LANES)]).reshape(1, LANES)
    bf = jnp.concatenate([b_fg, b_fg, bpad(LANES - 2 * GATE_LANES)]).reshape(1, LANES)
    sel = _mlstm_select()
    nblk = lru_w_a.shape[2]
    wa, wx = lru_w_a[layer], lru_w_x[layer]
    w4 = (0.5 * jnp.concatenate([wa[0], wx[0], wa[1], wx[1]], axis=2)).astype(BF16)
    ba = lru_b_a[layer].reshape(2, nblk, 1, LANES)
    bx = lru_b_x[layer].reshape(2, nblk, 1, LANES)
    b4 = 0.5 * jnp.concatenate([ba[0], bx[0], ba[1], bx[1]], axis=2)
    w_o = w_out[layer].astype(BF16)
    w_f1 = w_ffn_in[layer]
    w_f2 = w_ffn_out[layer].astype(BF16)

    n_rows = SUBLANES * ((bsz + 1 + SUBLANES - 1) // SUBLANES)
    c_rows = jnp.concatenate([c, c_ctx[None], jnp.zeros((n_rows - bsz - 1, d), c.dtype)], axis=0)
    mod3 = _modulation(c_rows, w_mod[layer], b_mod[layer]).reshape(n_rows, 1, 6 * d)

    x4 = x.reshape(bsz, rows, cols, d)
    tm = rows * COLS_PER_TILE
    tps = cols // COLS_PER_TILE
    qkv, wide, gates = _inproj(x4, g_pre_mix[layer], mod3, lambda i: i // tps,
                               w_big, w_gate, 3 * dm, tm, 1536)
    qkv_c, wide_c, gates_c = _inproj(ctx.reshape(bsz * t_ctx, d), g_pre_mix[layer], mod3,
                                     lambda i: bsz, w_big, w_gate, 3 * dm, tm, 1536)
    wide = wide.reshape(bsz, s, 3 * dm)
    wide_c = wide_c.reshape(bsz, t_ctx, 3 * dm)

    c0, m0 = _mlstm_ctx(qkv_c.reshape(bsz, t_ctx, 3 * dm),
                        gates_c.reshape(bsz, t_ctx, 2 * LANES), bi, bf, sel, dh)
    y_m = _mlstm_lat(qkv.reshape(bsz, s, 3 * dm), gates.reshape(bsz, s, 2 * LANES), wide,
                     bi, bf, sel, mh_norm_g[layer], c0, m0, dh)

    y_r = _rglru(wide, wide_c, conv_w[layer], conv_b[layer], w4, b4, lru_lambda[layer],
                 dm, rows, cols)

    x1, h2 = _outproj(y_m.reshape(bsz * s, dm), y_r.reshape(bsz * s, dm), w_o, x4,
                      g_post_mix[layer], g_pre_ffn[layer], mod3)
    act = _ffn_in(h2, w_f1, d_ff, 1024, 512)
    out = _ffn_out(act, w_f2, x1, g_post_ffn[layer], mod3, bsz, rows, cols, 2)
    return out.reshape(bsz, s, d)
```

```python
import functools

import jax
import jax.numpy as jnp
from jax import lax
from jax.experimental import pallas as pl
from jax.experimental.pallas import tpu as pltpu

F32 = jnp.float32
BF16 = jnp.bfloat16

GRID_W = 64
N_HEADS = 4
CHUNK = 256
LRU_C = 8.0
EPS = 1e-6
N_GATE_TYPES = 4
GATE_LANES = 2 * N_HEADS

LANES = 128
SUBLANES = 8
VMEM_LIMIT_BYTES = 56 * 1024 * 1024
INPROJ_VMEM_LIMIT_BYTES = 60 * 1024 * 1024

COLS_PER_TILE = SUBLANES


def _cparams(sem):
    return pltpu.CompilerParams(dimension_semantics=sem,
                                vmem_limit_bytes=VMEM_LIMIT_BYTES)


def _rms(xf, g):
    return xf * lax.rsqrt(jnp.mean(xf * xf, axis=-1, keepdims=True) + EPS) * g


def _sigmoid(x):
    return 1.0 / (1.0 + jnp.exp(-x))


def _sigmoid_tanh(x):
    return 0.5 * jnp.tanh(0.5 * x) + 0.5


def _softplus(x):
    return jnp.maximum(x, 0.0) + jnp.log1p(jnp.exp(-jnp.abs(x)))


def _dot(a, b):
    return jnp.dot(a, b, preferred_element_type=F32)


def _mod_kernel(c_ref, w_ref, b_ref, o_ref):
    cv = c_ref[...]
    o_ref[...] = _dot(cv * _sigmoid(cv), w_ref[...]) + b_ref[...]


def _modulation(c_rows, w_mod, b_mod, tn=512):
    r, d = c_rows.shape
    n = w_mod.shape[1]
    return pl.pallas_call(
        _mod_kernel,
        grid=(n // tn,),
        in_specs=[pl.BlockSpec((r, d), lambda j: (0, 0)),
                  pl.BlockSpec((d, tn), lambda j: (0, j)),
                  pl.BlockSpec((1, tn), lambda j: (0, j))],
        out_specs=pl.BlockSpec((r, tn), lambda j: (0, j)),
        out_shape=jax.ShapeDtypeStruct((r, n), F32),
        compiler_params=_cparams(("arbitrary",)),
        name="mod",
    )(c_rows, w_mod, b_mod.reshape(1, n))


def _grid_to_colmajor(x_blk):
    rows, ct, d = x_blk.shape
    return jnp.swapaxes(x_blk, 0, 1).reshape(ct * rows, d)


def _colmajor_to_grid(y, rows):
    n, d = y.shape
    return jnp.swapaxes(y.reshape(n // rows, rows, d), 0, 1)


def _dot_nt(a, b_t):
    return lax.dot_general(a, b_t, (((1,), (1,)), ((), ())), preferred_element_type=F32)


def _inproj_kernel(*refs, n_bf16_tiles, n_tiles, n_row_tiles, tn, col_tile):
    if col_tile:
        (x_ref, g_ref, shift_ref, scale_ref, wt_ref, wgt_ref,
         qkv_ref, wide_ref, gate_ref, xcm_ref, hx_ref) = refs
    else:
        (x_ref, g_ref, shift_ref, scale_ref, wt_ref, wgt_ref,
         qkv_ref, wide_ref, gate_ref, hx_ref) = refs
    i = pl.program_id(0)
    j = pl.program_id(1)
    both = jnp.logical_and

    def prepare(dst):
        if col_tile:
            xcm_ref[...] = _grid_to_colmajor(x_ref[0])
            xf = xcm_ref[...]
        else:
            xf = x_ref[...]
        hx_ref[dst] = (_rms(xf, g_ref[...]) * (1.0 + scale_ref[0]) + shift_ref[0]).astype(BF16)

    @pl.when(both(i == 0, j == 0))
    def _():
        prepare(0)

    for slot in (0, 1):
        mine = (i % 2) == slot

        def proj(slot=slot):
            r0 = pl.multiple_of(j * tn, tn)
            return _dot_nt(hx_ref[slot], wt_ref[pl.ds(r0, tn), :])

        @pl.when(both(mine, j == 0))
        def _(slot=slot):
            gate_ref[...] = _dot_nt(hx_ref[slot], wgt_ref[...])

        @pl.when(both(mine, j < n_bf16_tiles))
        def _(proj=proj):
            qkv_ref[...] = proj().astype(BF16)

        @pl.when(both(mine, both(j >= n_bf16_tiles, j < n_tiles - 1)))
        def _(proj=proj):
            wide_ref[...] = proj()

        @pl.when(both(mine, both(j == n_tiles - 1, i < n_row_tiles - 1)))
        def _(proj=proj, slot=slot):
            wide_ref[...] = proj()
            prepare(1 - slot)

        @pl.when(both(mine, both(j == n_tiles - 1, i == n_row_tiles - 1)))
        def _(proj=proj):
            wide_ref[...] = proj()


def _inproj(x, g, mod3, mod_row, wt_big, wt_gate, n_bf16, tm, tn):
    d = x.shape[-1]
    col_tile = x.ndim == 4
    n_tot = wt_big.shape[0]
    n_bt = n_bf16 // tn
    n_tiles = n_tot // tn
    n = x.size // d
    n_row_tiles = n // tm
    seen = lambda i, j: jnp.minimum(i + (j == n_tiles - 1).astype(jnp.int32), n_row_tiles - 1)
    if col_tile:
        bsz, rows, cols, _ = x.shape
        tps = cols // (tm // rows)
        x_spec = pl.BlockSpec((1, rows, tm // rows, d),
                              lambda i, j: (seen(i, j) // tps, 0, seen(i, j) % tps, 0))
    else:
        x_spec = pl.BlockSpec((tm, d), lambda i, j: (seen(i, j), 0))
    kern = functools.partial(_inproj_kernel, n_bf16_tiles=n_bt, n_tiles=n_tiles,
                             n_row_tiles=n_row_tiles, tn=tn, col_tile=col_tile)
    return pl.pallas_call(
        kern,
        grid=(n_row_tiles, n_tiles),
        in_specs=[x_spec,
                  pl.BlockSpec((1, d), lambda i, j: (0, 0)),
                  pl.BlockSpec((1, 1, d), lambda i, j: (mod_row(seen(i, j)), 0, 0)),
                  pl.BlockSpec((1, 1, d), lambda i, j: (mod_row(seen(i, j)), 0, 1)),
                  pl.BlockSpec((n_tot, d), lambda i, j: (0, 0), pipeline_mode=pl.Buffered(1)),
                  pl.BlockSpec((2 * LANES, d), lambda i, j: (0, 0), pipeline_mode=pl.Buffered(1))],
        out_specs=[pl.BlockSpec((tm, tn), lambda i, j: (i, jnp.minimum(j, n_bt - 1))),
                   pl.BlockSpec((tm, tn), lambda i, j: (i, jnp.maximum(j - n_bt, 0))),
                   pl.BlockSpec((tm, 2 * LANES), lambda i, j: (i, 0))]
        + ([pl.BlockSpec((tm, d), lambda i, j: (seen(i, j), 0))] if col_tile else []),
        out_shape=[jax.ShapeDtypeStruct((n, n_bf16), BF16),
                   jax.ShapeDtypeStruct((n, n_tot - n_bf16), F32),
                   jax.ShapeDtypeStruct((n, 2 * LANES), F32)]
        + ([jax.ShapeDtypeStruct((n, d), F32)] if col_tile else []),
        scratch_shapes=[pltpu.VMEM((2, tm, d), BF16)],
        compiler_params=pltpu.CompilerParams(dimension_semantics=("arbitrary", "arbitrary"),
                                             vmem_limit_bytes=INPROJ_VMEM_LIMIT_BYTES),
        name="inproj",
    )(x, g.reshape(1, d), mod3, mod3, wt_big, wt_gate)


def _keep_mask(n, rev):
    r = lax.broadcasted_iota(jnp.int32, (n, n), 0)
    c = lax.broadcasted_iota(jnp.int32, (n, n), 1)
    return (c >= r) if rev else (c <= r)


def _split3(x):
    hi = x.astype(BF16)
    r1 = x - hi.astype(F32)
    mid = r1.astype(BF16)
    lo = (r1 - mid.astype(F32)).astype(BF16)
    return hi, mid, lo


def _split2(x):
    hi = x.astype(BF16)
    return hi, (x - hi.astype(F32)).astype(BF16)


def _lane_tile(x, width):
    return jnp.concatenate([x] * (width // x.shape[1]), axis=1)


def _gate_prep(gi, gf, bi, bf, rev):
    n = gi.shape[0]
    ig = gi + bi
    xf = gf + bf
    lf = jnp.minimum(xf, 0.0) - jnp.log1p(jnp.exp(-jnp.abs(xf)))
    tri = _keep_mask(n, rev).astype(BF16)
    h, m, l = _split3(lf)
    cum = _dot(tri, h) + _dot(tri, m) + _dot(tri, l)
    lane = lax.broadcasted_iota(jnp.int32, cum.shape, 1)
    z = jnp.where(lane < GATE_LANES, cum, ig - cum)
    return z, z.T


def _mlstm_state_terms(k, v_aug, cb, igcb, m_prev, rev):
    n = k.shape[0]
    tot = cb[0:1, :] if rev else cb[n - 1:n, :]
    w_log = tot + igcb
    m_new = jnp.maximum(tot + m_prev, jnp.max(w_log, axis=0, keepdims=True))
    a = jnp.exp(tot + m_prev - m_new)
    w = jnp.exp(w_log - m_new).astype(BF16)
    wk = k * _lane_tile(w, k.shape[1])
    ct_loc = lax.dot_general(wk, v_aug, (((0,), (0,)), ((), ())), preferred_element_type=F32)
    return a, ct_loc, m_new


def _mlstm_decay_weights(s, cb, igc_r, m_prev, keep):
    n = s.shape[0]
    log_d = jnp.where(keep, _lane_tile(cb, n) + igc_r, -jnp.inf)
    log_inter = cb + m_prev
    m_row = jnp.maximum(log_inter, jnp.max(log_d, axis=1, keepdims=True))
    dmat = jnp.exp(log_d - _lane_tile(m_row, n))
    return (s * dmat).astype(BF16), jnp.exp(log_inter - m_row).astype(BF16), jnp.exp(-m_row)


def _mlstm_kernel(*refs, nc, dh, dm, emit_h):
    if emit_h:
        (qkv_ref, g_ref, o_ref, bi_ref, bf_ref, sel_ref, mhg_ref, c0_ref, m0_ref,
         y_ref, ct_ref, m_ref, hf_ref) = refs
    else:
        (qkv_ref, g_ref, bi_ref, bf_ref, sel_ref, c_out, m_out, ct_ref, m_ref) = refs
    p = pl.program_id(1)
    t = pl.program_id(2)

    @pl.when(t == 0)
    def _():
        if emit_h:
            ct_ref[...] = c0_ref[0, 0]
            m_ref[...] = m0_ref[0, 0]
        else:
            ct_ref[...] = jnp.zeros_like(ct_ref)
            m_ref[...] = jnp.zeros_like(m_ref)

    def run(rev):
        z, z_t = _gate_prep(g_ref[0, :, 0:LANES], g_ref[0, :, LANES:2 * LANES],
                            bi_ref[...], bf_ref[...], rev)
        zh, zl = _split2(z)
        keep = _keep_mask(CHUNK, rev)
        ones = jnp.ones((CHUNK, LANES), BF16)
        heads = range(N_HEADS)
        lanes = [h + N_HEADS * int(rev) for h in heads]
        hsl = [slice(h * dh, (h + 1) * dh) for h in heads]
        q = [qkv_ref[0, :, hsl[h]] for h in heads]
        k = [qkv_ref[0, :, dm + h * dh:dm + (h + 1) * dh] for h in heads]
        v_aug = [jnp.concatenate([qkv_ref[0, :, 2 * dm + h * dh:2 * dm + (h + 1) * dh], ones],
                                 axis=1) for h in heads]
        m_prev = [m_ref[h, 0:1, :] for h in heads]
        if emit_h:
            s = [lax.dot_general(q[h], k[h], (((1,), (1,)), ((), ())),
                                 preferred_element_type=F32) for h in heads]
        cols = [_dot(zh, sel_ref[l]) + _dot(zl, sel_ref[l]) for l in lanes]
        cb = [c[:, :LANES] for c in cols]
        igcb = [c[:, LANES:] for c in cols]
        upd = [_mlstm_state_terms(k[h], v_aug[h], cb[h], igcb[h], m_prev[h], rev) for h in heads]
        if emit_h:
            wts = [_mlstm_decay_weights(s[h], cb[h], z_t[GATE_LANES + lanes[h]:GATE_LANES + lanes[h] + 1, :],
                                        m_prev[h], keep) for h in heads]
            num = [_dot(wts[h][0], v_aug[h])
                   + _dot(q[h] * _lane_tile(wts[h][1], dh), ct_ref[h].astype(BF16)) for h in heads]
            for h in heads:
                r = 1.0 / jnp.maximum(jnp.abs(num[h][:, dh:]), wts[h][2])
                hout = num[h][:, :dh] * _lane_tile(r, dh)
                if not rev:
                    hf_ref[t, :, hsl[h]] = hout
                else:
                    hsum = hout + hf_ref[nc - 1 - t, :, hsl[h]]
                    hn = hsum * lax.rsqrt(jnp.mean(hsum * hsum, axis=1, keepdims=True) + EPS)
                    y_ref[0, :, hsl[h]] = (_sigmoid(o_ref[0, :, hsl[h]])
                                           * (hn * mhg_ref[:, hsl[h]])).astype(BF16)
        for h in heads:
            a, ct_loc, m_new = upd[h]
            ct_ref[h] = _lane_tile(a, dh + LANES) * ct_ref[h] + ct_loc
            m_ref[h] = jnp.broadcast_to(m_new, m_ref.shape[1:])

    @pl.when(p == 0)
    def _():
        run(False)

    @pl.when(p == 1)
    def _():
        run(True)

    if not emit_h:
        @pl.when(t == nc - 1)
        def _():
            c_out[0, 0] = ct_ref[...]
            m_out[0, 0] = m_ref[...]


def _mlstm_state_shapes(dh):
    return [(N_HEADS, dh, dh + LANES), (N_HEADS, SUBLANES, LANES)]


def _mlstm_select():
    row = lax.broadcasted_iota(jnp.int32, (GATE_LANES, LANES, 2 * LANES), 1)
    col = lax.broadcasted_iota(jnp.int32, (GATE_LANES, LANES, 2 * LANES), 2)
    lane = lax.broadcasted_iota(jnp.int32, (GATE_LANES, LANES, 2 * LANES), 0)
    return (row == jnp.where(col < LANES, lane, lane + GATE_LANES)).astype(BF16)


def _mlstm_ctx(qkv, gates, bi, bf, sel, dh):
    bsz, t, w3 = qkv.shape
    dm = w3 // 3
    nc = t // CHUNK
    chunk = lambda p, s: s + p * (nc - 1 - 2 * s)
    kern = functools.partial(_mlstm_kernel, nc=nc, dh=dh, dm=dm, emit_h=False)
    st = lambda shape: pl.BlockSpec((1, 1) + shape, lambda b, p, s: (b, p) + (0,) * len(shape))
    shapes = _mlstm_state_shapes(dh)
    return pl.pallas_call(
        kern,
        grid=(bsz, 2, nc),
        in_specs=[pl.BlockSpec((1, CHUNK, w3), lambda b, p, s: (b, chunk(p, s), 0)),
                  pl.BlockSpec((1, CHUNK, 2 * LANES), lambda b, p, s: (b, chunk(p, s), 0)),
                  pl.BlockSpec((1, LANES), lambda b, p, s: (0, 0)),
                  pl.BlockSpec((1, LANES), lambda b, p, s: (0, 0)),
                  pl.BlockSpec(sel.shape, lambda b, p, s: (0, 0, 0))],
        out_specs=[st(sh) for sh in shapes],
        out_shape=[jax.ShapeDtypeStruct((bsz, 2) + sh, F32) for sh in shapes],
        scratch_shapes=[pltpu.VMEM(sh, F32) for sh in shapes],
        compiler_params=_cparams(("arbitrary",) * 3),
        name="mlstm_ctx",
    )(qkv, gates, bi, bf, sel)


def _mlstm_lat(qkv, gates, wide, bi, bf, sel, mh_g, c0, m0, dh):
    bsz, s, w3 = qkv.shape
    dm = w3 // 3
    nc = s // CHUNK
    chunk = lambda p, t: t + p * (nc - 1 - 2 * t)
    ochunk = lambda p, t: nc - 1 - p * t
    kern = functools.partial(_mlstm_kernel, nc=nc, dh=dh, dm=dm, emit_h=True)
    st = lambda shape: pl.BlockSpec((1, 1) + shape, lambda b, p, t: (b, p) + (0,) * len(shape))
    shapes = _mlstm_state_shapes(dh)
    return pl.pallas_call(
        kern,
        grid=(bsz, 2, nc),
        in_specs=[pl.BlockSpec((1, CHUNK, w3), lambda b, p, t: (b, chunk(p, t), 0)),
                  pl.BlockSpec((1, CHUNK, 2 * LANES), lambda b, p, t: (b, chunk(p, t), 0)),
                  pl.BlockSpec((1, CHUNK, dm), lambda b, p, t: (b, ochunk(p, t), 0)),
                  pl.BlockSpec((1, LANES), lambda b, p, t: (0, 0)),
                  pl.BlockSpec((1, LANES), lambda b, p, t: (0, 0)),
                  pl.BlockSpec(sel.shape, lambda b, p, t: (0, 0, 0)),
                  pl.BlockSpec((1, dm), lambda b, p, t: (0, 0))]
        + [st(sh) for sh in shapes],
        out_specs=pl.BlockSpec((1, CHUNK, dm), lambda b, p, t: (b, ochunk(p, t), 0)),
        out_shape=jax.ShapeDtypeStruct((bsz, s, dm), BF16),
        scratch_shapes=[pltpu.VMEM(sh, F32) for sh in shapes]
        + [pltpu.VMEM((nc, CHUNK, dm), F32)],
        compiler_params=_cparams(("arbitrary",) * 3),
        name="mlstm_lat",
    )(qkv, gates, wide, bi, bf, sel, mh_g.reshape(1, dm), c0, m0)


_LRU_TILE = 512
_CONV_TAPS_BEFORE = 2
_F32_TINY = float(jnp.finfo(jnp.float32).tiny)


def _gelu_tanh(x):
    return x * (0.5 * (1.0 + jnp.tanh(0.7978845608028654 * (x + 0.044715 * (x * x * x)))))


def _block_scan(a, u, carry, row8, rev):
    for d in (1, 2, 4):
        sh = SUBLANES - d if rev else d
        m = (row8 < SUBLANES - d) if rev else (row8 >= d)
        u = a * jnp.where(m, pltpu.roll(u, sh, 0), 0.0) + u
        a = a * jnp.where(m, pltpu.roll(a, sh, 0), 1.0)
    sh = SUBLANES - 1 if rev else 1
    m = (row8 < SUBLANES - 1) if rev else (row8 >= 1)
    h_after = u + a * carry
    h_before = jnp.where(m, pltpu.roll(h_after, sh, 0), carry)
    last = h_after[0:1, :] if rev else h_after[SUBLANES - 1:SUBLANES, :]
    return h_before, h_after, last


def _rglru_kernel(xr_ref, gr_ref, xctx_ref, cw_ref, cb_ref, w4_ref, b4_ref, lam_ref, y_ref,
                  xp_ref, af_ref, uf_ref, ab_ref, ub_ref, *, rows, cols, t_ctx):
    blk = LANES
    s = rows * cols
    cw = cw_ref[...]
    cb = cb_ref[...]
    w4 = w4_ref[0]
    b4 = b4_ref[0]
    sp = _softplus(-lam_ref[...])
    row8 = lax.broadcasted_iota(jnp.int32, (SUBLANES, blk), 0)
    before = _CONV_TAPS_BEFORE

    half_c_sp = (0.5 * LRU_C) * sp

    def gates_to_scratch(xc, r0, n):
        half_xc = 0.5 * xc

        def a_u(r_half, i_half, c_row):
            tr = jnp.tanh(r_half)
            neg_log_a = c_row * tr + c_row
            a = jnp.exp(-neg_log_a)
            one_minus_a2 = jnp.tanh(neg_log_a) * (a * a + 1.0)
            root = one_minus_a2 * lax.rsqrt(jnp.maximum(one_minus_a2, _F32_TINY))
            ti = jnp.tanh(i_half)
            return a, root * (half_xc * ti + half_xc)
        z = _dot(xc.astype(BF16), w4) + b4
        a, u = a_u(z[:, 0:blk], z[:, blk:2 * blk], half_c_sp[0:1])
        af_ref[pl.ds(r0, n), :] = a
        uf_ref[pl.ds(r0, n), :] = u
        a, u = a_u(z[:, 2 * blk:3 * blk], z[:, 3 * blk:4 * blk], half_c_sp[1:2])
        ab_ref[pl.ds(r0, n), :] = a
        ub_ref[pl.ds(r0, n), :] = u

    pad = SUBLANES
    zeros = jnp.zeros((pad, blk), F32)
    xp_ref[0:pad, :] = zeros
    xp_ref[pad:pad + t_ctx, :] = xctx_ref[0]
    xp_ref[pad + t_ctx:2 * pad + t_ctx, :] = zeros
    ext = xp_ref[0:t_ctx + 2 * pad, :]
    xc = cb
    for j in range(cw.shape[0]):
        xc = xc + cw[j:j + 1] * ext[pad - before + j:pad - before + j + t_ctx]
    gates_to_scratch(xc, 0, t_ctx)
    nb = t_ctx // SUBLANES

    def ctx_body(i, carry):
        cf, cbw = carry
        rf = pl.multiple_of(i * SUBLANES, SUBLANES)
        rb = pl.multiple_of((nb - 1 - i) * SUBLANES, SUBLANES)
        _, _, cf = _block_scan(af_ref[pl.ds(rf, SUBLANES), :], uf_ref[pl.ds(rf, SUBLANES), :],
                               cf, row8, False)
        _, _, cbw = _block_scan(ab_ref[pl.ds(rb, SUBLANES), :], ub_ref[pl.ds(rb, SUBLANES), :],
                                cbw, row8, True)
        return cf, cbw
    zero = jnp.zeros((1, blk), F32)
    h0f, h0b = lax.fori_loop(0, nb, ctx_body, (zero, zero), unroll=4)

    rowv = lax.broadcasted_iota(jnp.int32, (rows, blk), 0)

    def copy(i, c):
        r0 = pl.multiple_of(i * _LRU_TILE, _LRU_TILE)
        xp_ref[pl.ds(before * rows + r0, _LRU_TILE), :] = xr_ref[0, pl.ds(r0, _LRU_TILE), :]
        return c
    lax.fori_loop(0, s // _LRU_TILE, copy, 0)
    for k in range(before):
        src = (cols - before + k) * rows
        ext = xr_ref[0, src - SUBLANES:src + rows, :]
        xp_ref[k * rows:(k + 1) * rows, :] = jnp.where(rowv == 0, 0.0,
                                                       ext[SUBLANES - 1:SUBLANES - 1 + rows])
    ext = xr_ref[0, 0:rows + SUBLANES, :]
    xp_ref[(cols + before) * rows:(cols + before + 1) * rows, :] = jnp.where(
        rowv == rows - 1, 0.0, ext[1:1 + rows])

    def fill(i, c):
        r0 = pl.multiple_of(i * _LRU_TILE, _LRU_TILE)
        xc = cb
        for j in range(cw.shape[0]):
            xc = xc + cw[j:j + 1] * xp_ref[pl.ds(r0 + j * rows, _LRU_TILE), :]
        gates_to_scratch(xc, r0, _LRU_TILE)
        return c
    lax.fori_loop(0, s // _LRU_TILE, fill, 0)

    def pass1(i, carry):
        hf, pf, hb, pb = carry
        rf = pl.multiple_of(i * rows, rows)
        rb = pl.multiple_of((cols - 1 - i) * rows, rows)
        a = af_ref[pl.ds(rf, rows), :]
        hf = a * hf + uf_ref[pl.ds(rf, rows), :]
        pf = pf * a
        uf_ref[pl.ds(rf, rows), :] = hf
        af_ref[pl.ds(rf, rows), :] = pf
        a = ab_ref[pl.ds(rb, rows), :]
        hb = a * hb + ub_ref[pl.ds(rb, rows), :]
        pb = pb * a
        ub_ref[pl.ds(rb, rows), :] = hb
        ab_ref[pl.ds(rb, rows), :] = pb
        return hf, pf, hb, pb
    z64 = jnp.zeros((rows, blk), F32)
    o64 = jnp.ones((rows, blk), F32)
    hf_end, pf_end, hb_end, pb_end = lax.fori_loop(0, cols, pass1, (z64, o64, z64, o64), unroll=2)

    nrb = rows // SUBLANES
    carry = h0f
    hin_f = []
    for k in range(nrb):
        sl = slice(k * SUBLANES, (k + 1) * SUBLANES)
        before_k, _, carry = _block_scan(pf_end[sl], hf_end[sl], carry, row8, False)
        hin_f.append(before_k)
    carry = h0b
    hin_b = [None] * nrb
    for k in reversed(range(nrb)):
        sl = slice(k * SUBLANES, (k + 1) * SUBLANES)
        before_k, _, carry = _block_scan(pb_end[sl], hb_end[sl], carry, row8, True)
        hin_b[k] = before_k
    hin_f = jnp.concatenate(hin_f, axis=0)
    hin_b = jnp.concatenate(hin_b, axis=0)

    def emit(i, c):
        r0 = pl.multiple_of(i * rows, rows)
        hsum = (uf_ref[pl.ds(r0, rows), :] + af_ref[pl.ds(r0, rows), :] * hin_f
                + ub_ref[pl.ds(r0, rows), :] + ab_ref[pl.ds(r0, rows), :] * hin_b)
        y_ref[0, pl.ds(r0, rows), :] = (_gelu_tanh(gr_ref[0, pl.ds(r0, rows), :]) * hsum).astype(BF16)
        return c
    lax.fori_loop(0, cols, emit, 0, unroll=2)


def _rglru(wide, wide_ctx, conv_w, conv_b, w4, b4, lam, dm, rows, cols):
    bsz, s, _ = wide.shape
    t_ctx = wide_ctx.shape[1]
    nblk = dm // LANES
    assert conv_w.shape[0] == _CONV_TAPS_BEFORE + 2 and s == rows * cols
    kern = functools.partial(_rglru_kernel, rows=rows, cols=cols, t_ctx=t_ctx)
    col = lambda shape, off: pl.BlockSpec(shape, lambda b, n: (b, 0, off + n))
    par = lambda shape: pl.BlockSpec(shape, lambda b, n: (0,) * (len(shape) - 1) + (n,))
    return pl.pallas_call(
        kern,
        grid=(bsz, nblk),
        in_specs=[col((1, s, LANES), nblk), col((1, s, LANES), 2 * nblk),
                  col((1, t_ctx, LANES), nblk),
                  par((conv_w.shape[0], LANES)), par((1, LANES)),
                  pl.BlockSpec((1, LANES, 4 * LANES), lambda b, n: (n, 0, 0)),
                  pl.BlockSpec((1, 1, 4 * LANES), lambda b, n: (n, 0, 0)),
                  par((2, LANES))],
        out_specs=pl.BlockSpec((1, s, LANES), lambda b, n: (b, 0, n)),
        out_shape=jax.ShapeDtypeStruct((bsz, s, dm), BF16),
        scratch_shapes=[pltpu.VMEM((s + (conv_w.shape[0] - 1) * rows, LANES), F32)]
        + [pltpu.VMEM((s, LANES), F32)] * 4,
        compiler_params=_cparams(("arbitrary", "arbitrary")),
        name="rglru",
    )(wide, wide, wide_ctx, conv_w, conv_b.reshape(1, dm), w4, b4, lam)


def _outproj_kernel(ym_ref, yr_ref, w_ref, x_ref, gpost_ref, gpre_ref,
                    gate_ref, scale_ref, shift_ref, x1_ref, h2_ref, y_scr, *, dm, n_row_tiles):
    i = pl.program_id(0)
    both = jnp.logical_and

    def project(slot):
        y_scr[slot] = (_dot(ym_ref[...], w_ref[0:dm, :]) + _dot(yr_ref[...], w_ref[dm:2 * dm, :]))

    def finish(slot):
        x1 = x_ref[...] + gate_ref[0] * _rms(y_scr[slot], gpost_ref[...])
        x1_ref[...] = x1
        h2 = _rms(x1, gpre_ref[...]) * (1.0 + scale_ref[0]) + shift_ref[0]
        h2_ref[...] = h2.astype(BF16)

    for slot in (0, 1):
        mine = (i % 2) == slot

        @pl.when(both(mine, i == 0))
        def _(slot=slot):
            project(slot)

        @pl.when(both(mine, both(i > 0, i < n_row_tiles)))
        def _(slot=slot):
            finish(1 - slot)
            project(slot)

        @pl.when(both(mine, i == n_row_tiles))
        def _(slot=slot):
            finish(1 - slot)


def _outproj(ym, yr, w_out, x_cm, g_post, g_pre, mod3, tm, tps):
    n, d = x_cm.shape
    dm = ym.shape[1]
    nt = n // tm
    kern = functools.partial(_outproj_kernel, dm=dm, n_row_tiles=nt)
    cur = lambda i: jnp.minimum(i, nt - 1)
    prev = lambda i: jnp.maximum(i - 1, 0)
    modspec = lambda k: pl.BlockSpec((1, 1, d), lambda i: (prev(i) // tps, 0, k))
    vec = pl.BlockSpec((1, d), lambda i: (0, 0))
    return pl.pallas_call(
        kern,
        grid=(nt + 1,),
        in_specs=[pl.BlockSpec((tm, dm), lambda i: (cur(i), 0)),
                  pl.BlockSpec((tm, dm), lambda i: (cur(i), 0)),
                  pl.BlockSpec((2 * dm, d), lambda i: (0, 0), pipeline_mode=pl.Buffered(1)),
                  pl.BlockSpec((tm, d), lambda i: (prev(i), 0)),
                  vec, vec, modspec(2), modspec(4), modspec(3)],
        out_specs=[pl.BlockSpec((tm, d), lambda i: (prev(i), 0)),
                   pl.BlockSpec((tm, d), lambda i: (prev(i), 0))],
        out_shape=[jax.ShapeDtypeStruct((n, d), F32), jax.ShapeDtypeStruct((n, d), BF16)],
        scratch_shapes=[pltpu.VMEM((2, tm, d), F32)],
        compiler_params=_cparams(("arbitrary",)),
        name="outproj",
    )(ym, yr, w_out, x_cm, g_post.reshape(1, d), g_pre.reshape(1, d), mod3, mod3, mod3)


def _ffn_in_kernel(h_ref, wg_ref, wu_ref, o_ref):
    h = h_ref[...]
    g = _dot(h, wg_ref[...].astype(BF16))
    u = _dot(h, wu_ref[...].astype(BF16))
    o_ref[...] = ((g * _sigmoid(g)) * u).astype(BF16)


def _ffn_in(h2, w_in, d_ff, tm, tn):
    n, d = h2.shape
    nj = d_ff // tn
    return pl.pallas_call(
        _ffn_in_kernel,
        grid=(n // tm, nj),
        in_specs=[pl.BlockSpec((tm, d), lambda i, j: (i, 0)),
                  pl.BlockSpec((d, tn), lambda i, j: (0, j)),
                  pl.BlockSpec((d, tn), lambda i, j: (0, j + nj))],
        out_specs=pl.BlockSpec((tm, tn), lambda i, j: (i, j)),
        out_shape=jax.ShapeDtypeStruct((n, d_ff), BF16),
        compiler_params=_cparams(("arbitrary", "arbitrary")),
        name="ffn_in",
    )(h2, w_in, w_in)


def _ffn_out_kernel(a_ref, w_ref, x1_ref, g_ref, gate_ref, o_ref, acc_ref, fin_ref,
                    *, n_row_tiles, n_k):
    i = pl.program_id(0)
    k = pl.program_id(1)
    live = i < n_row_tiles

    def finish():
        res = x1_ref[...] + gate_ref[0] * _rms(fin_ref[(i + 1) % 2], g_ref[...])
        o_ref[0] = _colmajor_to_grid(res, o_ref.shape[1])

    def partial_dot():
        return _dot(a_ref[...], w_ref[...])

    @pl.when(jnp.logical_and(k == 0, i == 0))
    def _():
        acc_ref[...] = partial_dot()

    @pl.when(jnp.logical_and(k == 0, jnp.logical_and(i > 0, live)))
    def _():
        finish()
        acc_ref[...] = partial_dot()

    @pl.when(jnp.logical_and(k == 0, i == n_row_tiles))
    def _():
        finish()

    @pl.when(jnp.logical_and(live, jnp.logical_and(k > 0, k < n_k - 1)))
    def _():
        acc_ref[...] += partial_dot()

    @pl.when(jnp.logical_and(live, k == n_k - 1))
    def _():
        fin_ref[i % 2] = acc_ref[...] + partial_dot()


def _ffn_out(act, w_out, x1, g_post, mod3, bsz, rows, cols, n_k):
    n, d_ff = act.shape
    d = x1.shape[1]
    tm = rows * COLS_PER_TILE
    tps = cols // COLS_PER_TILE
    nt = n // tm
    tk = d_ff // n_k
    assert n_k >= 3 and tk % LANES == 0
    kern = functools.partial(_ffn_out_kernel, n_row_tiles=nt, n_k=n_k)
    cur = lambda i: jnp.minimum(i, nt - 1)
    prev = lambda i: jnp.maximum(i - 1, 0)
    kk = lambda i, k: jnp.where(i < nt, k, n_k - 1)
    return pl.pallas_call(
        kern,
        grid=(nt + 1, n_k),
        in_specs=[pl.BlockSpec((tm, tk), lambda i, k: (cur(i), kk(i, k))),
                  pl.BlockSpec((tk, d), lambda i, k: (kk(i, k), 0)),
                  pl.BlockSpec((tm, d), lambda i, k: (prev(i), 0)),
                  pl.BlockSpec((1, d), lambda i, k: (0, 0)),
                  pl.BlockSpec((1, 1, d), lambda i, k: (prev(i) // tps, 0, 5))],
        out_specs=pl.BlockSpec((1, rows, COLS_PER_TILE, d),
                               lambda i, k: (prev(i) // tps, 0, prev(i) % tps, 0)),
        out_shape=jax.ShapeDtypeStruct((bsz, rows, cols, d), F32),
        scratch_shapes=[pltpu.VMEM((tm, d), F32), pltpu.VMEM((2, tm, d), F32)],
        compiler_params=_cparams(("arbitrary", "arbitrary")),
        name="ffn_out",
    )(act, w_out, x1, g_post.reshape(1, d), mod3)


def kernel(x, c, ctx, c_ctx, w_mod, b_mod, g_pre_mix, g_post_mix, g_pre_ffn, g_post_ffn,
           w_in, b_gates, mh_norm_g, conv_w, conv_b, lru_w_a, lru_b_a, lru_w_x, lru_b_x,
           lru_lambda, w_out, w_ffn_in, w_ffn_out):
    bsz, s, d = x.shape
    t_ctx = ctx.shape[1]
    depth = w_mod.shape[0]
    assert depth == 1, "context-stream update between layers is not implemented"
    cols = GRID_W
    rows = s // cols
    dm = mh_norm_g.shape[1]
    assert dm == conv_w.shape[2]
    dh = dm // N_HEADS
    n_gates = N_GATE_TYPES * N_HEADS
    d_ff = w_ffn_out.shape[1]
    layer = 0

    wt = jnp.swapaxes(w_in[layer], 0, 1)
    o4 = 4 * dm
    o5 = o4 + n_gates
    wt_big = jnp.concatenate([wt[:dm], wt[dm:2 * dm] * (dh ** -0.5), wt[2 * dm:o4], wt[o5:]],
                             axis=0).astype(BF16)
    wgt = wt[o4:o5].reshape(N_GATE_TYPES, N_HEADS, d)
    w_ig = jnp.concatenate([wgt[0], wgt[2]], axis=0)
    w_fg = jnp.concatenate([wgt[1], wgt[3]], axis=0)
    zpad = lambda n: jnp.zeros((n, d), wt.dtype)
    wt_gate = jnp.concatenate([zpad(GATE_LANES), w_ig, zpad(LANES - 2 * GATE_LANES),
                               w_fg, w_fg, zpad(LANES - 2 * GATE_LANES)], axis=0).astype(BF16)
    bg = b_gates[layer].astype(F32)
    b_ig = jnp.concatenate([bg[0], bg[2]])
    b_fg = jnp.concatenate([bg[1], bg[3]])
    bpad = lambda n: jnp.zeros((n,), F32)
    bi = jnp.concatenate([bpad(GATE_LANES), b_ig, bpad(LANES - 2 * GATE_LANES)]).reshape(1, LANES)
    bf = jnp.concatenate([b_fg, b_fg, bpad(LANES - 2 * GATE_LANES)]).reshape(1, LANES)
    sel = _mlstm_select()
    nblk = lru_w_a.shape[2]
    wa, wx = lru_w_a[layer], lru_w_x[layer]
    w4 = (0.5 * jnp.concatenate([wa[0], wx[0], wa[1], wx[1]], axis=2)).astype(BF16)
    ba = lru_b_a[layer].reshape(2, nblk, 1, LANES)
    bx = lru_b_x[layer].reshape(2, nblk, 1, LANES)
    b4 = 0.5 * jnp.concatenate([ba[0], bx[0], ba[1], bx[1]], axis=2)
    w_o = w_out[layer].astype(BF16)
    w_f1 = w_ffn_in[layer]
    w_f2 = w_ffn_out[layer].astype(BF16)

    n_rows = SUBLANES * ((bsz + 1 + SUBLANES - 1) // SUBLANES)
    c_rows = jnp.concatenate([c, c_ctx[None], jnp.zeros((n_rows - bsz - 1, d), c.dtype)], axis=0)
    mod3 = _modulation(c_rows, w_mod[layer], b_mod[layer]).reshape(n_rows, 1, 6 * d)

    x4 = x.reshape(bsz, rows, cols, d)
    tm = rows * COLS_PER_TILE
    tps = cols // COLS_PER_TILE
    qkv, wide, gates, x_cm = _inproj(x4, g_pre_mix[layer], mod3, lambda i: i // tps,
                                     wt_big, wt_gate, 3 * dm, tm, 1024)
    qkv_c, wide_c, gates_c = _inproj(ctx.reshape(bsz * t_ctx, d), g_pre_mix[layer], mod3,
                                     lambda i: bsz, wt_big, wt_gate, 3 * dm, tm, 1024)
    wide = wide.reshape(bsz, s, 3 * dm)
    wide_c = wide_c.reshape(bsz, t_ctx, 3 * dm)

    c0, m0 = _mlstm_ctx(qkv_c.reshape(bsz, t_ctx, 3 * dm),
                        gates_c.reshape(bsz, t_ctx, 2 * LANES), bi, bf, sel, dh)
    y_m = _mlstm_lat(qkv.reshape(bsz, s, 3 * dm), gates.reshape(bsz, s, 2 * LANES), wide,
                     bi, bf, sel, mh_norm_g[layer], c0, m0, dh)

    y_r = _rglru(wide, wide_c, conv_w[layer], conv_b[layer], w4, b4, lru_lambda[layer],
                 dm, rows, cols)

    x1, h2 = _outproj(y_m.reshape(bsz * s, dm), y_r.reshape(bsz * s, dm), w_o, x_cm,
                      g_post_mix[layer], g_pre_ffn[layer], mod3, tm, tps)
    act = _ffn_in(h2, w_f1, d_ff, 1024, 512)
    out = _ffn_out(act, w_f2, x1, g_post_ffn[layer], mod3, bsz, rows, cols, 4)
    return out.reshape(bsz, s, d)
```

```python
import functools

import jax
import jax.numpy as jnp
from jax import lax
from jax.experimental import pallas as pl
from jax.experimental.pallas import tpu as pltpu

F32 = jnp.float32
BF16 = jnp.bfloat16

GRID_W = 64
N_HEADS = 4
CHUNK = 256
LRU_C = 8.0
EPS = 1e-6
N_GATE_TYPES = 4
GATE_LANES = 2 * N_HEADS

LANES = 128
SUBLANES = 8
VMEM_LIMIT_BYTES = 56 * 1024 * 1024
INPROJ_VMEM_LIMIT_BYTES = 60 * 1024 * 1024

COLS_PER_TILE = SUBLANES


def _cparams(sem):
    return pltpu.CompilerParams(dimension_semantics=sem,
                                vmem_limit_bytes=VMEM_LIMIT_BYTES)


def _rms(xf, g):
    return xf * lax.rsqrt(jnp.mean(xf * xf, axis=-1, keepdims=True) + EPS) * g


def _sigmoid(x):
    return 1.0 / (1.0 + jnp.exp(-x))


def _sigmoid_tanh(x):
    return 0.5 * jnp.tanh(0.5 * x) + 0.5


def _softplus(x):
    return jnp.maximum(x, 0.0) + jnp.log1p(jnp.exp(-jnp.abs(x)))


def _dot(a, b):
    return jnp.dot(a, b, preferred_element_type=F32)


def _mod_kernel(c_ref, w_ref, b_ref, o_ref):
    cv = c_ref[...]
    o_ref[...] = _dot(cv * _sigmoid(cv), w_ref[...]) + b_ref[...]


def _modulation(c_rows, w_mod, b_mod, tn=512):
    r, d = c_rows.shape
    n = w_mod.shape[1]
    return pl.pallas_call(
        _mod_kernel,
        grid=(n // tn,),
        in_specs=[pl.BlockSpec((r, d), lambda j: (0, 0)),
                  pl.BlockSpec((d, tn), lambda j: (0, j)),
                  pl.BlockSpec((1, tn), lambda j: (0, j))],
        out_specs=pl.BlockSpec((r, tn), lambda j: (0, j)),
        out_shape=jax.ShapeDtypeStruct((r, n), F32),
        compiler_params=_cparams(("arbitrary",)),
        name="mod",
    )(c_rows, w_mod, b_mod.reshape(1, n))


def _grid_to_colmajor(x_blk):
    rows, ct, d = x_blk.shape
    return jnp.swapaxes(x_blk, 0, 1).reshape(ct * rows, d)


def _colmajor_to_grid(y, rows):
    n, d = y.shape
    return jnp.swapaxes(y.reshape(n // rows, rows, d), 0, 1)


def _dot_nt(a, b_t):
    return lax.dot_general(a, b_t, (((1,), (1,)), ((), ())), preferred_element_type=F32)


def _grid_tile_copies(grid_hbm, tile_ref, sem, tile, tiles_per_sample, to_grid):
    rows = grid_hbm.shape[1]
    b = tile // tiles_per_sample
    c0 = (tile % tiles_per_sample) * COLS_PER_TILE
    copies = []
    for cc in range(COLS_PER_TILE):
        hbm = grid_hbm.at[b, :, c0 + cc, :]
        vmem = tile_ref.at[pl.ds(cc * rows, rows), :]
        src, dst = (vmem, hbm) if to_grid else (hbm, vmem)
        copies.append(pltpu.make_async_copy(src, dst, sem.at[cc]))
    return copies


def _inproj_kernel(*refs, n_bf16_tiles, n_tiles, n_row_tiles, tn, tiles_per_sample):
    col_tile = tiles_per_sample is not None
    if col_tile:
        (x_hbm, g_ref, shift_ref, scale_ref, wt_ref, wgt_ref,
         qkv_ref, wide_ref, gate_ref, xcm_ref, hx_ref, xt_ref, sem) = refs
        gather = lambda tile: _grid_tile_copies(x_hbm, xt_ref, sem, tile, tiles_per_sample, False)
    else:
        (x_ref, g_ref, shift_ref, scale_ref, wt_ref, wgt_ref,
         qkv_ref, wide_ref, gate_ref, hx_ref) = refs
    i = pl.program_id(0)
    j = pl.program_id(1)
    both = jnp.logical_and

    def prepare(dst):
        if col_tile:
            xf = xt_ref[...]
            xcm_ref[...] = xf
        else:
            xf = x_ref[...]
        hx_ref[dst] = (_rms(xf, g_ref[...]) * (1.0 + scale_ref[0]) + shift_ref[0]).astype(BF16)

    @pl.when(both(i == 0, j == 0))
    def _():
        if col_tile:
            for cp in gather(0):
                cp.start()
            for cp in gather(0):
                cp.wait()
        prepare(0)

    if col_tile:
        @pl.when(both(j == 0, i < n_row_tiles - 1))
        def _():
            for cp in gather(i + 1):
                cp.start()

    for slot in (0, 1):
        mine = (i % 2) == slot

        def proj(slot=slot):
            r0 = pl.multiple_of(j * tn, tn)
            return _dot_nt(hx_ref[slot], wt_ref[pl.ds(r0, tn), :])

        @pl.when(both(mine, j == 0))
        def _(slot=slot):
            gate_ref[...] = _dot_nt(hx_ref[slot], wgt_ref[...])

        @pl.when(both(mine, j < n_bf16_tiles))
        def _(proj=proj):
            qkv_ref[...] = proj().astype(BF16)

        @pl.when(both(mine, both(j >= n_bf16_tiles, j < n_tiles - 1)))
        def _(proj=proj):
            wide_ref[...] = proj()

        @pl.when(both(mine, both(j == n_tiles - 1, i < n_row_tiles - 1)))
        def _(proj=proj, slot=slot):
            if col_tile:
                for cp in gather(i + 1):
                    cp.wait()
            wide_ref[...] = proj()
            prepare(1 - slot)

        @pl.when(both(mine, both(j == n_tiles - 1, i == n_row_tiles - 1)))
        def _(proj=proj):
            wide_ref[...] = proj()


def _inproj(x, g, mod3, mod_row, wt_big, wt_gate, n_bf16, tm, tn):
    d = x.shape[-1]
    col_tile = x.ndim == 4
    n_tot = wt_big.shape[0]
    n_bt = n_bf16 // tn
    n_tiles = n_tot // tn
    n = x.size // d
    n_row_tiles = n // tm
    seen = lambda i, j: jnp.minimum(i + (j == n_tiles - 1).astype(jnp.int32), n_row_tiles - 1)
    if col_tile:
        bsz, rows, cols, _ = x.shape
        assert tm == rows * COLS_PER_TILE
        tps = cols // COLS_PER_TILE
        x_spec = pl.BlockSpec(memory_space=pl.ANY)
        extra_scratch = [pltpu.VMEM((tm, d), F32), pltpu.SemaphoreType.DMA((COLS_PER_TILE,))]
    else:
        tps = None
        x_spec = pl.BlockSpec((tm, d), lambda i, j: (seen(i, j), 0))
        extra_scratch = []
    kern = functools.partial(_inproj_kernel, n_bf16_tiles=n_bt, n_tiles=n_tiles,
                             n_row_tiles=n_row_tiles, tn=tn, tiles_per_sample=tps)
    return pl.pallas_call(
        kern,
        grid=(n_row_tiles, n_tiles),
        in_specs=[x_spec,
                  pl.BlockSpec((1, d), lambda i, j: (0, 0)),
                  pl.BlockSpec((1, 1, d), lambda i, j: (mod_row(seen(i, j)), 0, 0)),
                  pl.BlockSpec((1, 1, d), lambda i, j: (mod_row(seen(i, j)), 0, 1)),
                  pl.BlockSpec((n_tot, d), lambda i, j: (0, 0), pipeline_mode=pl.Buffered(1)),
                  pl.BlockSpec((2 * LANES, d), lambda i, j: (0, 0), pipeline_mode=pl.Buffered(1))],
        out_specs=[pl.BlockSpec((tm, tn), lambda i, j: (i, jnp.minimum(j, n_bt - 1))),
                   pl.BlockSpec((tm, tn), lambda i, j: (i, jnp.maximum(j - n_bt, 0))),
                   pl.BlockSpec((tm, 2 * LANES), lambda i, j: (i, 0))]
        + ([pl.BlockSpec((tm, d), lambda i, j: (seen(i, j), 0))] if col_tile else []),
        out_shape=[jax.ShapeDtypeStruct((n, n_bf16), BF16),
                   jax.ShapeDtypeStruct((n, n_tot - n_bf16), F32),
                   jax.ShapeDtypeStruct((n, 2 * LANES), F32)]
        + ([jax.ShapeDtypeStruct((n, d), F32)] if col_tile else []),
        scratch_shapes=[pltpu.VMEM((2, tm, d), BF16)] + extra_scratch,
        compiler_params=pltpu.CompilerParams(dimension_semantics=("arbitrary", "arbitrary"),
                                             vmem_limit_bytes=INPROJ_VMEM_LIMIT_BYTES),
        name="inproj",
    )(x, g.reshape(1, d), mod3, mod3, wt_big, wt_gate)


def _keep_mask(n, rev):
    r = lax.broadcasted_iota(jnp.int32, (n, n), 0)
    c = lax.broadcasted_iota(jnp.int32, (n, n), 1)
    return (c >= r) if rev else (c <= r)


def _split3(x):
    hi = x.astype(BF16)
    r1 = x - hi.astype(F32)
    mid = r1.astype(BF16)
    lo = (r1 - mid.astype(F32)).astype(BF16)
    return hi, mid, lo


def _split2(x):
    hi = x.astype(BF16)
    return hi, (x - hi.astype(F32)).astype(BF16)


def _lane_tile(x, width):
    return jnp.concatenate([x] * (width // x.shape[1]), axis=1)


def _gate_prep(gi, gf, bi, bf, rev):
    n = gi.shape[0]
    ig = gi + bi
    xf = gf + bf
    lf = jnp.minimum(xf, 0.0) - jnp.log1p(jnp.exp(-jnp.abs(xf)))
    tri = _keep_mask(n, rev).astype(BF16)
    h, m, l = _split3(lf)
    cum = _dot(tri, h) + _dot(tri, m) + _dot(tri, l)
    lane = lax.broadcasted_iota(jnp.int32, cum.shape, 1)
    z = jnp.where(lane < GATE_LANES, cum, ig - cum)
    return z, z.T


def _mlstm_state_terms(k, v_aug, cb, igcb, m_prev, rev):
    n = k.shape[0]
    tot = cb[0:1, :] if rev else cb[n - 1:n, :]
    w_log = tot + igcb
    m_new = jnp.maximum(tot + m_prev, jnp.max(w_log, axis=0, keepdims=True))
    a = jnp.exp(tot + m_prev - m_new)
    w = jnp.exp(w_log - m_new).astype(BF16)
    wk = k * _lane_tile(w, k.shape[1])
    ct_loc = lax.dot_general(wk, v_aug, (((0,), (0,)), ((), ())), preferred_element_type=F32)
    return a, ct_loc, m_new


def _mlstm_decay_weights(s, cb, igc_r, m_prev, keep):
    n = s.shape[0]
    log_d = jnp.where(keep, _lane_tile(cb, n) + igc_r, -jnp.inf)
    log_inter = cb + m_prev
    m_row = jnp.maximum(log_inter, jnp.max(log_d, axis=1, keepdims=True))
    dmat = jnp.exp(log_d - _lane_tile(m_row, n))
    return (s * dmat).astype(BF16), jnp.exp(log_inter - m_row).astype(BF16), jnp.exp(-m_row)


def _mlstm_kernel(*refs, nc, dh, dm, emit_h):
    if emit_h:
        (qkv_ref, g_ref, o_ref, bi_ref, bf_ref, sel_ref, mhg_ref, c0_ref, m0_ref,
         y_ref, ct_ref, m_ref, hf_ref) = refs
    else:
        (qkv_ref, g_ref, bi_ref, bf_ref, sel_ref, c_out, m_out, ct_ref, m_ref) = refs
    p = pl.program_id(1)
    t = pl.program_id(2)

    @pl.when(t == 0)
    def _():
        if emit_h:
            ct_ref[...] = c0_ref[0, 0]
            m_ref[...] = m0_ref[0, 0]
        else:
            ct_ref[...] = jnp.zeros_like(ct_ref)
            m_ref[...] = jnp.zeros_like(m_ref)

    def run(rev):
        z, z_t = _gate_prep(g_ref[0, :, 0:LANES], g_ref[0, :, LANES:2 * LANES],
                            bi_ref[...], bf_ref[...], rev)
        zh, zl = _split2(z)
        keep = _keep_mask(CHUNK, rev)
        ones = jnp.ones((CHUNK, LANES), BF16)
        heads = range(N_HEADS)
        lanes = [h + N_HEADS * int(rev) for h in heads]
        hsl = [slice(h * dh, (h + 1) * dh) for h in heads]
        q = [qkv_ref[0, :, hsl[h]] for h in heads]
        k = [qkv_ref[0, :, dm + h * dh:dm + (h + 1) * dh] for h in heads]
        v_aug = [jnp.concatenate([qkv_ref[0, :, 2 * dm + h * dh:2 * dm + (h + 1) * dh], ones],
                                 axis=1) for h in heads]
        m_prev = [m_ref[h, 0:1, :] for h in heads]
        if emit_h:
            s = [lax.dot_general(q[h], k[h], (((1,), (1,)), ((), ())),
                                 preferred_element_type=F32) for h in heads]
        cols = [_dot(zh, sel_ref[l]) + _dot(zl, sel_ref[l]) for l in lanes]
        cb = [c[:, :LANES] for c in cols]
        igcb = [c[:, LANES:] for c in cols]
        upd = [_mlstm_state_terms(k[h], v_aug[h], cb[h], igcb[h], m_prev[h], rev) for h in heads]
        if emit_h:
            wts = [_mlstm_decay_weights(s[h], cb[h], z_t[GATE_LANES + lanes[h]:GATE_LANES + lanes[h] + 1, :],
                                        m_prev[h], keep) for h in heads]
            num = [_dot(wts[h][0], v_aug[h])
                   + _dot(q[h] * _lane_tile(wts[h][1], dh), ct_ref[h].astype(BF16)) for h in heads]
            for h in heads:
                r = 1.0 / jnp.maximum(jnp.abs(num[h][:, dh:]), wts[h][2])
                hout = num[h][:, :dh] * _lane_tile(r, dh)
                if not rev:
                    hf_ref[t, :, hsl[h]] = hout
                else:
                    hsum = hout + hf_ref[nc - 1 - t, :, hsl[h]]
                    hn = hsum * lax.rsqrt(jnp.mean(hsum * hsum, axis=1, keepdims=True) + EPS)
                    y_ref[0, :, hsl[h]] = (_sigmoid(o_ref[0, :, hsl[h]])
                                           * (hn * mhg_ref[:, hsl[h]])).astype(BF16)
        for h in heads:
            a, ct_loc, m_new = upd[h]
            ct_ref[h] = _lane_tile(a, dh + LANES) * ct_ref[h] + ct_loc
            m_ref[h] = jnp.broadcast_to(m_new, m_ref.shape[1:])

    @pl.when(p == 0)
    def _():
        run(False)

    @pl.when(p == 1)
    def _():
        run(True)

    if not emit_h:
        @pl.when(t == nc - 1)
        def _():
            c_out[0, 0] = ct_ref[...]
            m_out[0, 0] = m_ref[...]


def _mlstm_state_shapes(dh):
    return [(N_HEADS, dh, dh + LANES), (N_HEADS, SUBLANES, LANES)]


def _mlstm_select():
    row = lax.broadcasted_iota(jnp.int32, (GATE_LANES, LANES, 2 * LANES), 1)
    col = lax.broadcasted_iota(jnp.int32, (GATE_LANES, LANES, 2 * LANES), 2)
    lane = lax.broadcasted_iota(jnp.int32, (GATE_LANES, LANES, 2 * LANES), 0)
    return (row == jnp.where(col < LANES, lane, lane + GATE_LANES)).astype(BF16)


def _mlstm_ctx(qkv, gates, bi, bf, sel, dh):
    bsz, t, w3 = qkv.shape
    dm = w3 // 3
    nc = t // CHUNK
    chunk = lambda p, s: s + p * (nc - 1 - 2 * s)
    kern = functools.partial(_mlstm_kernel, nc=nc, dh=dh, dm=dm, emit_h=False)
    st = lambda shape: pl.BlockSpec((1, 1) + shape, lambda b, p, s: (b, p) + (0,) * len(shape))
    shapes = _mlstm_state_shapes(dh)
    return pl.pallas_call(
        kern,
        grid=(bsz, 2, nc),
        in_specs=[pl.BlockSpec((1, CHUNK, w3), lambda b, p, s: (b, chunk(p, s), 0)),
                  pl.BlockSpec((1, CHUNK, 2 * LANES), lambda b, p, s: (b, chunk(p, s), 0)),
                  pl.BlockSpec((1, LANES), lambda b, p, s: (0, 0)),
                  pl.BlockSpec((1, LANES), lambda b, p, s: (0, 0)),
                  pl.BlockSpec(sel.shape, lambda b, p, s: (0, 0, 0))],
        out_specs=[st(sh) for sh in shapes],
        out_shape=[jax.ShapeDtypeStruct((bsz, 2) + sh, F32) for sh in shapes],
        scratch_shapes=[pltpu.VMEM(sh, F32) for sh in shapes],
        compiler_params=_cparams(("arbitrary",) * 3),
        name="mlstm_ctx",
    )(qkv, gates, bi, bf, sel)


def _mlstm_lat(qkv, gates, wide, bi, bf, sel, mh_g, c0, m0, dh):
    bsz, s, w3 = qkv.shape
    dm = w3 // 3
    nc = s // CHUNK
    chunk = lambda p, t: t + p * (nc - 1 - 2 * t)
    ochunk = lambda p, t: nc - 1 - p * t
    kern = functools.partial(_mlstm_kernel, nc=nc, dh=dh, dm=dm, emit_h=True)
    st = lambda shape: pl.BlockSpec((1, 1) + shape, lambda b, p, t: (b, p) + (0,) * len(shape))
    shapes = _mlstm_state_shapes(dh)
    return pl.pallas_call(
        kern,
        grid=(bsz, 2, nc),
        in_specs=[pl.BlockSpec((1, CHUNK, w3), lambda b, p, t: (b, chunk(p, t), 0)),
                  pl.BlockSpec((1, CHUNK, 2 * LANES), lambda b, p, t: (b, chunk(p, t), 0)),
                  pl.BlockSpec((1, CHUNK, dm), lambda b, p, t: (b, ochunk(p, t), 0)),
                  pl.BlockSpec((1, LANES), lambda b, p, t: (0, 0)),
                  pl.BlockSpec((1, LANES), lambda b, p, t: (0, 0)),
                  pl.BlockSpec(sel.shape, lambda b, p, t: (0, 0, 0)),
                  pl.BlockSpec((1, dm), lambda b, p, t: (0, 0))]
        + [st(sh) for sh in shapes],
        out_specs=pl.BlockSpec((1, CHUNK, dm), lambda b, p, t: (b, ochunk(p, t), 0)),
        out_shape=jax.ShapeDtypeStruct((bsz, s, dm), BF16),
        scratch_shapes=[pltpu.VMEM(sh, F32) for sh in shapes]
        + [pltpu.VMEM((nc, CHUNK, dm), F32)],
        compiler_params=_cparams(("arbitrary",) * 3),
        name="mlstm_lat",
    )(qkv, gates, wide, bi, bf, sel, mh_g.reshape(1, dm), c0, m0)


_LRU_TILE = 512
_CONV_TAPS_BEFORE = 2
_F32_TINY = float(jnp.finfo(jnp.float32).tiny)


def _gelu_tanh(x):
    return x * (0.5 * (1.0 + jnp.tanh(0.7978845608028654 * (x + 0.044715 * (x * x * x)))))


def _block_scan(a, u, carry, row8, rev):
    for d in (1, 2, 4):
        sh = SUBLANES - d if rev else d
        m = (row8 < SUBLANES - d) if rev else (row8 >= d)
        u = a * jnp.where(m, pltpu.roll(u, sh, 0), 0.0) + u
        a = a * jnp.where(m, pltpu.roll(a, sh, 0), 1.0)
    sh = SUBLANES - 1 if rev else 1
    m = (row8 < SUBLANES - 1) if rev else (row8 >= 1)
    h_after = u + a * carry
    h_before = jnp.where(m, pltpu.roll(h_after, sh, 0), carry)
    last = h_after[0:1, :] if rev else h_after[SUBLANES - 1:SUBLANES, :]
    return h_before, h_after, last


def _rglru_kernel(xr_ref, gr_ref, xctx_ref, cw_ref, cb_ref, w4_ref, b4_ref, lam_ref, y_ref,
                  xp_ref, af_ref, uf_ref, ab_ref, ub_ref, *, rows, cols, t_ctx):
    blk = LANES
    s = rows * cols
    cw = cw_ref[...]
    cb = cb_ref[...]
    w4 = w4_ref[0]
    b4 = b4_ref[0]
    sp = _softplus(-lam_ref[...])
    row8 = lax.broadcasted_iota(jnp.int32, (SUBLANES, blk), 0)
    before = _CONV_TAPS_BEFORE

    half_c_sp = (0.5 * LRU_C) * sp

    def gates_to_scratch(xc, r0, n):
        half_xc = 0.5 * xc

        def a_u(r_half, i_half, c_row):
            tr = jnp.tanh(r_half)
            neg_log_a = c_row * tr + c_row
            a = jnp.exp(-neg_log_a)
            one_minus_a2 = jnp.tanh(neg_log_a) * (a * a + 1.0)
            root = one_minus_a2 * lax.rsqrt(jnp.maximum(one_minus_a2, _F32_TINY))
            ti = jnp.tanh(i_half)
            return a, root * (half_xc * ti + half_xc)
        z = _dot(xc.astype(BF16), w4) + b4
        a, u = a_u(z[:, 0:blk], z[:, blk:2 * blk], half_c_sp[0:1])
        af_ref[pl.ds(r0, n), :] = a
        uf_ref[pl.ds(r0, n), :] = u
        a, u = a_u(z[:, 2 * blk:3 * blk], z[:, 3 * blk:4 * blk], half_c_sp[1:2])
        ab_ref[pl.ds(r0, n), :] = a
        ub_ref[pl.ds(r0, n), :] = u

    pad = SUBLANES
    zeros = jnp.zeros((pad, blk), F32)
    xp_ref[0:pad, :] = zeros
    xp_ref[pad:pad + t_ctx, :] = xctx_ref[0]
    xp_ref[pad + t_ctx:2 * pad + t_ctx, :] = zeros
    ext = xp_ref[0:t_ctx + 2 * pad, :]
    xc = cb
    for j in range(cw.shape[0]):
        xc = xc + cw[j:j + 1] * ext[pad - before + j:pad - before + j + t_ctx]
    gates_to_scratch(xc, 0, t_ctx)
    nb = t_ctx // SUBLANES

    def ctx_body(i, carry):
        cf, cbw = carry
        rf = pl.multiple_of(i * SUBLANES, SUBLANES)
        rb = pl.multiple_of((nb - 1 - i) * SUBLANES, SUBLANES)
        _, _, cf = _block_scan(af_ref[pl.ds(rf, SUBLANES), :], uf_ref[pl.ds(rf, SUBLANES), :],
                               cf, row8, False)
        _, _, cbw = _block_scan(ab_ref[pl.ds(rb, SUBLANES), :], ub_ref[pl.ds(rb, SUBLANES), :],
                                cbw, row8, True)
        return cf, cbw
    zero = jnp.zeros((1, blk), F32)
    h0f, h0b = lax.fori_loop(0, nb, ctx_body, (zero, zero), unroll=4)

    rowv = lax.broadcasted_iota(jnp.int32, (rows, blk), 0)

    def copy(i, c):
        r0 = pl.multiple_of(i * _LRU_TILE, _LRU_TILE)
        xp_ref[pl.ds(before * rows + r0, _LRU_TILE), :] = xr_ref[0, pl.ds(r0, _LRU_TILE), :]
        return c
    lax.fori_loop(0, s // _LRU_TILE, copy, 0)
    for k in range(before):
        src = (cols - before + k) * rows
        ext = xr_ref[0, src - SUBLANES:src + rows, :]
        xp_ref[k * rows:(k + 1) * rows, :] = jnp.where(rowv == 0, 0.0,
                                                       ext[SUBLANES - 1:SUBLANES - 1 + rows])
    ext = xr_ref[0, 0:rows + SUBLANES, :]
    xp_ref[(cols + before) * rows:(cols + before + 1) * rows, :] = jnp.where(
        rowv == rows - 1, 0.0, ext[1:1 + rows])

    def fill(i, c):
        r0 = pl.multiple_of(i * _LRU_TILE, _LRU_TILE)
        xc = cb
        for j in range(cw.shape[0]):
            xc = xc + cw[j:j + 1] * xp_ref[pl.ds(r0 + j * rows, _LRU_TILE), :]
        gates_to_scratch(xc, r0, _LRU_TILE)
        return c
    lax.fori_loop(0, s // _LRU_TILE, fill, 0)

    def pass1(i, carry):
        hf, pf, hb, pb = carry
        rf = pl.multiple_of(i * rows, rows)
        rb = pl.multiple_of((cols - 1 - i) * rows, rows)
        a = af_ref[pl.ds(rf, rows), :]
        hf = a * hf + uf_ref[pl.ds(rf, rows), :]
        pf = pf * a
        uf_ref[pl.ds(rf, rows), :] = hf
        af_ref[pl.ds(rf, rows), :] = pf
        a = ab_ref[pl.ds(rb, rows), :]
        hb = a * hb + ub_ref[pl.ds(rb, rows), :]
        pb = pb * a
        ub_ref[pl.ds(rb, rows), :] = hb
        ab_ref[pl.ds(rb, rows), :] = pb
        return hf, pf, hb, pb
    z64 = jnp.zeros((rows, blk), F32)
    o64 = jnp.ones((rows, blk), F32)
    hf_end, pf_end, hb_end, pb_end = lax.fori_loop(0, cols, pass1, (z64, o64, z64, o64), unroll=2)

    nrb = rows // SUBLANES
    carry = h0f
    hin_f = []
    for k in range(nrb):
        sl = slice(k * SUBLANES, (k + 1) * SUBLANES)
        before_k, _, carry = _block_scan(pf_end[sl], hf_end[sl], carry, row8, False)
        hin_f.append(before_k)
    carry = h0b
    hin_b = [None] * nrb
    for k in reversed(range(nrb)):
        sl = slice(k * SUBLANES, (k + 1) * SUBLANES)
        before_k, _, carry = _block_scan(pb_end[sl], hb_end[sl], carry, row8, True)
        hin_b[k] = before_k
    hin_f = jnp.concatenate(hin_f, axis=0)
    hin_b = jnp.concatenate(hin_b, axis=0)

    def emit(i, c):
        r0 = pl.multiple_of(i * rows, rows)
        hsum = (uf_ref[pl.ds(r0, rows), :] + af_ref[pl.ds(r0, rows), :] * hin_f
                + ub_ref[pl.ds(r0, rows), :] + ab_ref[pl.ds(r0, rows), :] * hin_b)
        y_ref[0, pl.ds(r0, rows), :] = (_gelu_tanh(gr_ref[0, pl.ds(r0, rows), :]) * hsum).astype(BF16)
        return c
    lax.fori_loop(0, cols, emit, 0, unroll=2)


def _rglru(wide, wide_ctx, conv_w, conv_b, w4, b4, lam, dm, rows, cols):
    bsz, s, _ = wide.shape
    t_ctx = wide_ctx.shape[1]
    nblk = dm // LANES
    assert conv_w.shape[0] == _CONV_TAPS_BEFORE + 2 and s == rows * cols
    kern = functools.partial(_rglru_kernel, rows=rows, cols=cols, t_ctx=t_ctx)
    col = lambda shape, off: pl.BlockSpec(shape, lambda b, n: (b, 0, off + n))
    par = lambda shape: pl.BlockSpec(shape, lambda b, n: (0,) * (len(shape) - 1) + (n,))
    return pl.pallas_call(
        kern,
        grid=(bsz, nblk),
        in_specs=[col((1, s, LANES), nblk), col((1, s, LANES), 2 * nblk),
                  col((1, t_ctx, LANES), nblk),
                  par((conv_w.shape[0], LANES)), par((1, LANES)),
                  pl.BlockSpec((1, LANES, 4 * LANES), lambda b, n: (n, 0, 0)),
                  pl.BlockSpec((1, 1, 4 * LANES), lambda b, n: (n, 0, 0)),
                  par((2, LANES))],
        out_specs=pl.BlockSpec((1, s, LANES), lambda b, n: (b, 0, n)),
        out_shape=jax.ShapeDtypeStruct((bsz, s, dm), BF16),
        scratch_shapes=[pltpu.VMEM((s + (conv_w.shape[0] - 1) * rows, LANES), F32)]
        + [pltpu.VMEM((s, LANES), F32)] * 4,
        compiler_params=_cparams(("arbitrary", "arbitrary")),
        name="rglru",
    )(wide, wide, wide_ctx, conv_w, conv_b.reshape(1, dm), w4, b4, lam)


def _outproj_kernel(ym_ref, yr_ref, w_ref, x_ref, gpost_ref, gpre_ref,
                    gate_ref, scale_ref, shift_ref, x1_ref, h2_ref, y_scr, *, dm, n_row_tiles):
    i = pl.program_id(0)
    both = jnp.logical_and

    def project(slot):
        y_scr[slot] = (_dot(ym_ref[...], w_ref[0:dm, :]) + _dot(yr_ref[...], w_ref[dm:2 * dm, :]))

    def finish(slot):
        x1 = x_ref[...] + gate_ref[0] * _rms(y_scr[slot], gpost_ref[...])
        x1_ref[...] = x1
        h2 = _rms(x1, gpre_ref[...]) * (1.0 + scale_ref[0]) + shift_ref[0]
        h2_ref[...] = h2.astype(BF16)

    for slot in (0, 1):
        mine = (i % 2) == slot

        @pl.when(both(mine, i == 0))
        def _(slot=slot):
            project(slot)

        @pl.when(both(mine, both(i > 0, i < n_row_tiles)))
        def _(slot=slot):
            finish(1 - slot)
            project(slot)

        @pl.when(both(mine, i == n_row_tiles))
        def _(slot=slot):
            finish(1 - slot)


def _outproj(ym, yr, w_out, x_cm, g_post, g_pre, mod3, tm, tps):
    n, d = x_cm.shape
    dm = ym.shape[1]
    nt = n // tm
    kern = functools.partial(_outproj_kernel, dm=dm, n_row_tiles=nt)
    cur = lambda i: jnp.minimum(i, nt - 1)
    prev = lambda i: jnp.maximum(i - 1, 0)
    modspec = lambda k: pl.BlockSpec((1, 1, d), lambda i: (prev(i) // tps, 0, k))
    vec = pl.BlockSpec((1, d), lambda i: (0, 0))
    return pl.pallas_call(
        kern,
        grid=(nt + 1,),
        in_specs=[pl.BlockSpec((tm, dm), lambda i: (cur(i), 0)),
                  pl.BlockSpec((tm, dm), lambda i: (cur(i), 0)),
                  pl.BlockSpec((2 * dm, d), lambda i: (0, 0), pipeline_mode=pl.Buffered(1)),
                  pl.BlockSpec((tm, d), lambda i: (prev(i), 0)),
                  vec, vec, modspec(2), modspec(4), modspec(3)],
        out_specs=[pl.BlockSpec((tm, d), lambda i: (prev(i), 0)),
                   pl.BlockSpec((tm, d), lambda i: (prev(i), 0))],
        out_shape=[jax.ShapeDtypeStruct((n, d), F32), jax.ShapeDtypeStruct((n, d), BF16)],
        scratch_shapes=[pltpu.VMEM((2, tm, d), F32)],
        compiler_params=_cparams(("arbitrary",)),
        name="outproj",
    )(ym, yr, w_out, x_cm, g_post.reshape(1, d), g_pre.reshape(1, d), mod3, mod3, mod3)


def _ffn_in_kernel(h_ref, wg_ref, wu_ref, o_ref):
    h = h_ref[...]
    g = _dot(h, wg_ref[...].astype(BF16))
    u = _dot(h, wu_ref[...].astype(BF16))
    o_ref[...] = ((g * _sigmoid(g)) * u).astype(BF16)


def _ffn_in(h2, w_in, d_ff, tm, tn):
    n, d = h2.shape
    nj = d_ff // tn
    return pl.pallas_call(
        _ffn_in_kernel,
        grid=(n // tm, nj),
        in_specs=[pl.BlockSpec((tm, d), lambda i, j: (i, 0)),
                  pl.BlockSpec((d, tn), lambda i, j: (0, j)),
                  pl.BlockSpec((d, tn), lambda i, j: (0, j + nj))],
        out_specs=pl.BlockSpec((tm, tn), lambda i, j: (i, j)),
        out_shape=jax.ShapeDtypeStruct((n, d_ff), BF16),
        compiler_params=_cparams(("arbitrary", "arbitrary")),
        name="ffn_in",
    )(h2, w_in, w_in)


def _ffn_out_kernel(a_ref, w_ref, x1_ref, g_ref, gate_ref, o_hbm, acc_ref, fin_ref, res_ref, sem,
                    *, n_row_tiles, tiles_per_sample):
    i = pl.program_id(0)
    k = pl.program_id(1)
    both = jnp.logical_and
    scatter = lambda tile: _grid_tile_copies(o_hbm, res_ref, sem, tile, tiles_per_sample, True)

    def finish():
        res_ref[...] = x1_ref[...] + gate_ref[0] * _rms(fin_ref[(i + 1) % 2], g_ref[...])
        for cp in scatter(i - 1):
            cp.start()

    def partial_dot():
        return _dot(a_ref[...], w_ref[...])

    @pl.when(both(k == 0, i >= 2))
    def _():
        for cp in scatter(i - 2):
            cp.wait()

    @pl.when(both(k == 0, i == 0))
    def _():
        acc_ref[...] = partial_dot()

    @pl.when(both(k == 0, both(i > 0, i < n_row_tiles)))
    def _():
        acc_ref[...] = partial_dot()
        finish()

    @pl.when(both(k == 0, i == n_row_tiles))
    def _():
        finish()

    @pl.when(both(k == 1, i < n_row_tiles))
    def _():
        fin_ref[i % 2] = acc_ref[...] + partial_dot()

    @pl.when(both(k == 1, i == n_row_tiles))
    def _():
        for cp in scatter(i - 1):
            cp.wait()


def _ffn_out(act, w_out, x1, g_post, mod3, bsz, rows, cols):
    n, d_ff = act.shape
    d = x1.shape[1]
    tm = rows * COLS_PER_TILE
    tps = cols // COLS_PER_TILE
    nt = n // tm
    n_k = 2
    tk = d_ff // n_k
    assert nt >= 2 and tk % LANES == 0
    kern = functools.partial(_ffn_out_kernel, n_row_tiles=nt, tiles_per_sample=tps)
    cur = lambda i: jnp.minimum(i, nt - 1)
    prev = lambda i: jnp.maximum(i - 1, 0)
    kk = lambda i, k: jnp.where(i < nt, k, n_k - 1)
    return pl.pallas_call(
        kern,
        grid=(nt + 1, n_k),
        in_specs=[pl.BlockSpec((tm, tk), lambda i, k: (cur(i), kk(i, k))),
                  pl.BlockSpec((tk, d), lambda i, k: (kk(i, k), 0)),
                  pl.BlockSpec((tm, d), lambda i, k: (prev(i), 0)),
                  pl.BlockSpec((1, d), lambda i, k: (0, 0)),
                  pl.BlockSpec((1, 1, d), lambda i, k: (prev(i) // tps, 0, 5))],
        out_specs=pl.BlockSpec(memory_space=pl.ANY),
        out_shape=jax.ShapeDtypeStruct((bsz, rows, cols, d), F32),
        scratch_shapes=[pltpu.VMEM((tm, d), F32), pltpu.VMEM((2, tm, d), F32),
                        pltpu.VMEM((tm, d), F32), pltpu.SemaphoreType.DMA((COLS_PER_TILE,))],
        compiler_params=_cparams(("arbitrary", "arbitrary")),
        name="ffn_out",
    )(act, w_out, x1, g_post.reshape(1, d), mod3)


def kernel(x, c, ctx, c_ctx, w_mod, b_mod, g_pre_mix, g_post_mix, g_pre_ffn, g_post_ffn,
           w_in, b_gates, mh_norm_g, conv_w, conv_b, lru_w_a, lru_b_a, lru_w_x, lru_b_x,
           lru_lambda, w_out, w_ffn_in, w_ffn_out):
    bsz, s, d = x.shape
    t_ctx = ctx.shape[1]
    depth = w_mod.shape[0]
    assert depth == 1, "context-stream update between layers is not implemented"
    cols = GRID_W
    rows = s // cols
    dm = mh_norm_g.shape[1]
    assert dm == conv_w.shape[2]
    dh = dm // N_HEADS
    n_gates = N_GATE_TYPES * N_HEADS
    d_ff = w_ffn_out.shape[1]
    layer = 0

    wt = jnp.swapaxes(w_in[layer], 0, 1)
    o4 = 4 * dm
    o5 = o4 + n_gates
    wt_big = jnp.concatenate([wt[:dm], wt[dm:2 * dm] * (dh ** -0.5), wt[2 * dm:o4], wt[o5:]],
                             axis=0).astype(BF16)
    wgt = wt[o4:o5].reshape(N_GATE_TYPES, N_HEADS, d)
    w_ig = jnp.concatenate([wgt[0], wgt[2]], axis=0)
    w_fg = jnp.concatenate([wgt[1], wgt[3]], axis=0)
    zpad = lambda n: jnp.zeros((n, d), wt.dtype)
    wt_gate = jnp.concatenate([zpad(GATE_LANES), w_ig, zpad(LANES - 2 * GATE_LANES),
                               w_fg, w_fg, zpad(LANES - 2 * GATE_LANES)], axis=0).astype(BF16)
    bg = b_gates[layer].astype(F32)
    b_ig = jnp.concatenate([bg[0], bg[2]])
    b_fg = jnp.concatenate([bg[1], bg[3]])
    bpad = lambda n: jnp.zeros((n,), F32)
    bi = jnp.concatenate([bpad(GATE_LANES), b_ig, bpad(LANES - 2 * GATE_LANES)]).reshape(1, LANES)
    bf = jnp.concatenate([b_fg, b_fg, bpad(LANES - 2 * GATE_LANES)]).reshape(1, LANES)
    sel = _mlstm_select()
    nblk = lru_w_a.shape[2]
    wa, wx = lru_w_a[layer], lru_w_x[layer]
    w4 = (0.5 * jnp.concatenate([wa[0], wx[0], wa[1], wx[1]], axis=2)).astype(BF16)
    ba = lru_b_a[layer].reshape(2, nblk, 1, LANES)
    bx = lru_b_x[layer].reshape(2, nblk, 1, LANES)
    b4 = 0.5 * jnp.concatenate([ba[0], bx[0], ba[1], bx[1]], axis=2)
    w_o = w_out[layer].astype(BF16)
    w_f1 = w_ffn_in[layer]
    w_f2 = w_ffn_out[layer].astype(BF16)

    n_rows = SUBLANES * ((bsz + 1 + SUBLANES - 1) // SUBLANES)
    c_rows = jnp.concatenate([c, c_ctx[None], jnp.zeros((n_rows - bsz - 1, d), c.dtype)], axis=0)
    mod3 = _modulation(c_rows, w_mod[layer], b_mod[layer]).reshape(n_rows, 1, 6 * d)

    x4 = x.reshape(bsz, rows, cols, d)
    tm = rows * COLS_PER_TILE
    tps = cols // COLS_PER_TILE
    qkv, wide, gates, x_cm = _inproj(x4, g_pre_mix[layer], mod3, lambda i: i // tps,
                                     wt_big, wt_gate, 3 * dm, tm, 1536)
    qkv_c, wide_c, gates_c = _inproj(ctx.reshape(bsz * t_ctx, d), g_pre_mix[layer], mod3,
                                     lambda i: bsz, wt_big, wt_gate, 3 * dm, tm, 1536)
    wide = wide.reshape(bsz, s, 3 * dm)
    wide_c = wide_c.reshape(bsz, t_ctx, 3 * dm)

    c0, m0 = _mlstm_ctx(qkv_c.reshape(bsz, t_ctx, 3 * dm),
                        gates_c.reshape(bsz, t_ctx, 2 * LANES), bi, bf, sel, dh)
    y_m = _mlstm_lat(qkv.reshape(bsz, s, 3 * dm), gates.reshape(bsz, s, 2 * LANES), wide,
                     bi, bf, sel, mh_norm_g[layer], c0, m0, dh)

    y_r = _rglru(wide, wide_c, conv_w[layer], conv_b[layer], w4, b4, lru_lambda[layer],
                 dm, rows, cols)

    x1, h2 = _outproj(y_m.reshape(bsz * s, dm), y_r.reshape(bsz * s, dm), w_o, x_cm,
                      g_post_mix[layer], g_pre_ffn[layer], mod3, tm, tps)
    act = _ffn_in(h2, w_f1, d_ff, 1024, 512)
    out = _ffn_out(act, w_f2, x1, g_post_ffn[layer], mod3, bsz, rows, cols)
    return out.reshape(bsz, s, d)
```

```python
import functools

import jax
import jax.numpy as jnp
from jax import lax
from jax.experimental import pallas as pl
from jax.experimental.pallas import tpu as pltpu

F32 = jnp.float32
BF16 = jnp.bfloat16

GRID_W = 64
N_HEADS = 4
CHUNK = 256
LRU_C = 8.0
EPS = 1e-6
N_GATE_TYPES = 4
GATE_LANES = 2 * N_HEADS

LANES = 128
SUBLANES = 8
VMEM_LIMIT_BYTES = 56 * 1024 * 1024
INPROJ_VMEM_LIMIT_BYTES = 60 * 1024 * 1024

COLS_PER_TILE = SUBLANES
GRID_DMA_PRIORITY = 1


def _cparams(sem):
    return pltpu.CompilerParams(dimension_semantics=sem,
                                vmem_limit_bytes=VMEM_LIMIT_BYTES)


def _rms(xf, g):
    return xf * lax.rsqrt(jnp.mean(xf * xf, axis=-1, keepdims=True) + EPS) * g


def _sigmoid(x):
    return 1.0 / (1.0 + jnp.exp(-x))


def _sigmoid_tanh(x):
    return 0.5 * jnp.tanh(0.5 * x) + 0.5


def _softplus(x):
    return jnp.maximum(x, 0.0) + jnp.log1p(jnp.exp(-jnp.abs(x)))


def _dot(a, b):
    return jnp.dot(a, b, preferred_element_type=F32)


def _mod_kernel(c_ref, w_ref, b_ref, o_ref):
    cv = c_ref[...]
    o_ref[...] = _dot(cv * _sigmoid(cv), w_ref[...]) + b_ref[...]


def _modulation(c_rows, w_mod, b_mod, tn=512):
    r, d = c_rows.shape
    n = w_mod.shape[1]
    return pl.pallas_call(
        _mod_kernel,
        grid=(n // tn,),
        in_specs=[pl.BlockSpec((r, d), lambda j: (0, 0)),
                  pl.BlockSpec((d, tn), lambda j: (0, j)),
                  pl.BlockSpec((1, tn), lambda j: (0, j))],
        out_specs=pl.BlockSpec((r, tn), lambda j: (0, j)),
        out_shape=jax.ShapeDtypeStruct((r, n), F32),
        compiler_params=_cparams(("arbitrary",)),
        name="mod",
    )(c_rows, w_mod, b_mod.reshape(1, n))


def _grid_to_colmajor(x_blk):
    rows, ct, d = x_blk.shape
    return jnp.swapaxes(x_blk, 0, 1).reshape(ct * rows, d)


def _colmajor_to_grid(y, rows):
    n, d = y.shape
    return jnp.swapaxes(y.reshape(n // rows, rows, d), 0, 1)


def _dot_nt(a, b_t):
    return lax.dot_general(a, b_t, (((1,), (1,)), ((), ())), preferred_element_type=F32)


def _grid_tile_copies(grid_hbm, tile_ref, sem, tile, tiles_per_sample, to_grid):
    rows = grid_hbm.shape[1]
    b = tile // tiles_per_sample
    c0 = (tile % tiles_per_sample) * COLS_PER_TILE
    copies = []
    for cc in range(COLS_PER_TILE):
        hbm = grid_hbm.at[b, :, c0 + cc, :]
        vmem = tile_ref.at[pl.ds(cc * rows, rows), :]
        src, dst = (vmem, hbm) if to_grid else (hbm, vmem)
        copies.append(pltpu.make_async_copy(src, dst, sem.at[cc]))
    return copies


def _inproj_kernel(*refs, n_bf16_tiles, tiles, n_row_tiles, tiles_per_sample):
    col_tile = tiles_per_sample is not None
    if col_tile:
        (x_hbm, g_ref, shift_ref, scale_ref, wt_ref, wgt_ref,
         qkv_ref, wide_ref, gate_ref, xcm_ref, hx_ref, xt_ref, sem) = refs
        gather = lambda tile: _grid_tile_copies(x_hbm, xt_ref, sem, tile, tiles_per_sample, False)
    else:
        (x_ref, g_ref, shift_ref, scale_ref, wt_ref, wgt_ref,
         qkv_ref, wide_ref, gate_ref, hx_ref) = refs
    i = pl.program_id(0)
    j = pl.program_id(1)
    both = jnp.logical_and

    def prepare(dst):
        if col_tile:
            xf = xt_ref[...]
            xcm_ref[...] = xf
        else:
            xf = x_ref[...]
        hx_ref[dst] = (_rms(xf, g_ref[...]) * (1.0 + scale_ref[0]) + shift_ref[0]).astype(BF16)

    @pl.when(both(i == 0, j == 0))
    def _():
        if col_tile:
            for cp in gather(0):
                cp.start(priority=GRID_DMA_PRIORITY)
            for cp in gather(0):
                cp.wait()
        prepare(0)

    if col_tile:
        @pl.when(both(j == 0, i < n_row_tiles - 1))
        def _():
            for cp in gather(i + 1):
                cp.start(priority=GRID_DMA_PRIORITY)

    def project(slot, jt):
        out_ref = qkv_ref if jt < n_bf16_tiles else wide_ref
        c0 = 0
        for r0, nr in tiles[jt]:
            out_ref[:, c0:c0 + nr] = _dot_nt(hx_ref[slot], wt_ref[r0:r0 + nr, :]).astype(out_ref.dtype)
            c0 += nr

    last = len(tiles) - 1
    for slot in (0, 1):
        mine = (i % 2) == slot

        @pl.when(both(mine, j == 0))
        def _(slot=slot):
            gate_ref[...] = _dot_nt(hx_ref[slot], wgt_ref[...])

        for jt in range(last):
            @pl.when(both(mine, j == jt))
            def _(slot=slot, jt=jt):
                project(slot, jt)

        @pl.when(both(mine, both(j == last, i < n_row_tiles - 1)))
        def _(slot=slot):
            if col_tile:
                for cp in gather(i + 1):
                    cp.wait()
            project(slot, last)
            prepare(1 - slot)

        @pl.when(both(mine, both(j == last, i == n_row_tiles - 1)))
        def _(slot=slot):
            project(slot, last)


def _inproj(x, g, mod3, mod_row, wt_all, wt_gate, n_bf16, skip, tm, tn):
    d = x.shape[-1]
    col_tile = x.ndim == 4
    p_in = wt_all.shape[0]
    n_tot = p_in - skip[1]
    n_bt = n_bf16 // tn
    n_tiles = n_tot // tn
    assert n_bf16 % tn == 0 and n_tot % tn == 0

    def pieces(c0, c1):
        if c1 <= skip[0]:
            return ((c0, c1 - c0),)
        if c0 >= skip[0]:
            return ((c0 + skip[1], c1 - c0),)
        return ((c0, skip[0] - c0), (skip[0] + skip[1], c1 - skip[0]))
    tiles = tuple(pieces(t * tn, (t + 1) * tn) for t in range(n_tiles))
    assert all(r0 % 16 == 0 and nr % LANES == 0 for tile in tiles for r0, nr in tile)
    n = x.size // d
    n_row_tiles = n // tm
    seen = lambda i, j: jnp.minimum(i + (j == n_tiles - 1).astype(jnp.int32), n_row_tiles - 1)
    if col_tile:
        bsz, rows, cols, _ = x.shape
        assert tm == rows * COLS_PER_TILE
        tps = cols // COLS_PER_TILE
        x_spec = pl.BlockSpec(memory_space=pl.ANY)
        extra_scratch = [pltpu.VMEM((tm, d), F32), pltpu.SemaphoreType.DMA((COLS_PER_TILE,))]
    else:
        tps = None
        x_spec = pl.BlockSpec((tm, d), lambda i, j: (seen(i, j), 0))
        extra_scratch = []
    kern = functools.partial(_inproj_kernel, n_bf16_tiles=n_bt, tiles=tiles,
                             n_row_tiles=n_row_tiles, tiles_per_sample=tps)
    return pl.pallas_call(
        kern,
        grid=(n_row_tiles, n_tiles),
        in_specs=[x_spec,
                  pl.BlockSpec((1, d), lambda i, j: (0, 0)),
                  pl.BlockSpec((1, 1, d), lambda i, j: (mod_row(seen(i, j)), 0, 0)),
                  pl.BlockSpec((1, 1, d), lambda i, j: (mod_row(seen(i, j)), 0, 1)),
                  pl.BlockSpec((p_in, d), lambda i, j: (0, 0), pipeline_mode=pl.Buffered(1)),
                  pl.BlockSpec((2 * LANES, d), lambda i, j: (0, 0), pipeline_mode=pl.Buffered(1))],
        out_specs=[pl.BlockSpec((tm, tn), lambda i, j: (i, jnp.minimum(j, n_bt - 1))),
                   pl.BlockSpec((tm, tn), lambda i, j: (i, jnp.maximum(j - n_bt, 0))),
                   pl.BlockSpec((tm, 2 * LANES), lambda i, j: (i, 0))]
        + ([pl.BlockSpec((tm, d), lambda i, j: (seen(i, j), 0))] if col_tile else []),
        out_shape=[jax.ShapeDtypeStruct((n, n_bf16), BF16),
                   jax.ShapeDtypeStruct((n, n_tot - n_bf16), F32),
                   jax.ShapeDtypeStruct((n, 2 * LANES), F32)]
        + ([jax.ShapeDtypeStruct((n, d), F32)] if col_tile else []),
        scratch_shapes=[pltpu.VMEM((2, tm, d), BF16)] + extra_scratch,
        compiler_params=pltpu.CompilerParams(dimension_semantics=("arbitrary", "arbitrary"),
                                             vmem_limit_bytes=INPROJ_VMEM_LIMIT_BYTES),
        name="inproj",
    )(x, g.reshape(1, d), mod3, mod3, wt_all, wt_gate)


def _keep_mask(n, rev):
    r = lax.broadcasted_iota(jnp.int32, (n, n), 0)
    c = lax.broadcasted_iota(jnp.int32, (n, n), 1)
    return (c >= r) if rev else (c <= r)


def _split3(x):
    hi = x.astype(BF16)
    r1 = x - hi.astype(F32)
    mid = r1.astype(BF16)
    lo = (r1 - mid.astype(F32)).astype(BF16)
    return hi, mid, lo


def _split2(x):
    hi = x.astype(BF16)
    return hi, (x - hi.astype(F32)).astype(BF16)


def _lane_tile(x, width):
    return jnp.concatenate([x] * (width // x.shape[1]), axis=1)


def _gate_prep(gi, gf, bi, bf, rev):
    n = gi.shape[0]
    ig = gi + bi
    xf = gf + bf
    lf = jnp.minimum(xf, 0.0) - jnp.log1p(jnp.exp(-jnp.abs(xf)))
    tri = _keep_mask(n, rev).astype(BF16)
    h, m, l = _split3(lf)
    cum = _dot(tri, h) + _dot(tri, m) + _dot(tri, l)
    lane = lax.broadcasted_iota(jnp.int32, cum.shape, 1)
    z = jnp.where(lane < GATE_LANES, cum, ig - cum)
    return z, z.T


def _mlstm_state_terms(k, v_aug, cb, igcb, m_prev, rev):
    n = k.shape[0]
    tot = cb[0:1, :] if rev else cb[n - 1:n, :]
    w_log = tot + igcb
    m_new = jnp.maximum(tot + m_prev, jnp.max(w_log, axis=0, keepdims=True))
    a = jnp.exp(tot + m_prev - m_new)
    w = jnp.exp(w_log - m_new).astype(BF16)
    wk = k * _lane_tile(w, k.shape[1])
    ct_loc = lax.dot_general(wk, v_aug, (((0,), (0,)), ((), ())), preferred_element_type=F32)
    return a, ct_loc, m_new


def _mlstm_decay_weights(s, cb, igc_r, m_prev, keep):
    n = s.shape[0]
    log_d = jnp.where(keep, _lane_tile(cb, n) + igc_r, -jnp.inf)
    log_inter = cb + m_prev
    m_row = jnp.maximum(log_inter, jnp.max(log_d, axis=1, keepdims=True))
    dmat = jnp.exp(log_d - _lane_tile(m_row, n))
    return (s * dmat).astype(BF16), jnp.exp(log_inter - m_row).astype(BF16), jnp.exp(-m_row)


def _mlstm_kernel(*refs, nc, dh, dm, emit_h):
    if emit_h:
        (qkv_ref, g_ref, o_ref, bi_ref, bf_ref, sel_ref, mhg_ref, c0_ref, m0_ref,
         y_ref, ct_ref, m_ref, hf_ref) = refs
    else:
        (qkv_ref, g_ref, bi_ref, bf_ref, sel_ref, c_out, m_out, ct_ref, m_ref) = refs
    p = pl.program_id(1)
    t = pl.program_id(2)

    @pl.when(t == 0)
    def _():
        if emit_h:
            ct_ref[...] = c0_ref[0, 0]
            m_ref[...] = m0_ref[0, 0]
        else:
            ct_ref[...] = jnp.zeros_like(ct_ref)
            m_ref[...] = jnp.zeros_like(m_ref)

    def run(rev):
        z, z_t = _gate_prep(g_ref[0, :, 0:LANES], g_ref[0, :, LANES:2 * LANES],
                            bi_ref[...], bf_ref[...], rev)
        zh, zl = _split2(z)
        keep = _keep_mask(CHUNK, rev)
        ones = jnp.ones((CHUNK, LANES), BF16)
        heads = range(N_HEADS)
        lanes = [h + N_HEADS * int(rev) for h in heads]
        hsl = [slice(h * dh, (h + 1) * dh) for h in heads]
        q = [qkv_ref[0, :, hsl[h]] for h in heads]
        k = [qkv_ref[0, :, dm + h * dh:dm + (h + 1) * dh] * (dh ** -0.5) for h in heads]
        v_aug = [jnp.concatenate([qkv_ref[0, :, 2 * dm + h * dh:2 * dm + (h + 1) * dh], ones],
                                 axis=1) for h in heads]
        m_prev = [m_ref[h, 0:1, :] for h in heads]
        if emit_h:
            s = [lax.dot_general(q[h], k[h], (((1,), (1,)), ((), ())),
                                 preferred_element_type=F32) for h in heads]
        cols = [_dot(zh, sel_ref[l]) + _dot(zl, sel_ref[l]) for l in lanes]
        cb = [c[:, :LANES] for c in cols]
        igcb = [c[:, LANES:] for c in cols]
        upd = [_mlstm_state_terms(k[h], v_aug[h], cb[h], igcb[h], m_prev[h], rev) for h in heads]
        if emit_h:
            wts = [_mlstm_decay_weights(s[h], cb[h], z_t[GATE_LANES + lanes[h]:GATE_LANES + lanes[h] + 1, :],
                                        m_prev[h], keep) for h in heads]
            num = [_dot(wts[h][0], v_aug[h])
                   + _dot(q[h] * _lane_tile(wts[h][1], dh), ct_ref[h].astype(BF16)) for h in heads]
            for h in heads:
                r = 1.0 / jnp.maximum(jnp.abs(num[h][:, dh:]), wts[h][2])
                hout = num[h][:, :dh] * _lane_tile(r, dh)
                if not rev:
                    hf_ref[t, :, hsl[h]] = hout
                else:
                    hsum = hout + hf_ref[nc - 1 - t, :, hsl[h]]
                    hn = hsum * lax.rsqrt(jnp.mean(hsum * hsum, axis=1, keepdims=True) + EPS)
                    y_ref[0, :, hsl[h]] = (_sigmoid(o_ref[0, :, hsl[h]])
                                           * (hn * mhg_ref[:, hsl[h]])).astype(BF16)
        for h in heads:
            a, ct_loc, m_new = upd[h]
            ct_ref[h] = _lane_tile(a, dh + LANES) * ct_ref[h] + ct_loc
            m_ref[h] = jnp.broadcast_to(m_new, m_ref.shape[1:])

    @pl.when(p == 0)
    def _():
        run(False)

    @pl.when(p == 1)
    def _():
        run(True)

    if not emit_h:
        @pl.when(t == nc - 1)
        def _():
            c_out[0, 0] = ct_ref[...]
            m_out[0, 0] = m_ref[...]


def _mlstm_state_shapes(dh):
    return [(N_HEADS, dh, dh + LANES), (N_HEADS, SUBLANES, LANES)]


def _mlstm_select():
    row = lax.broadcasted_iota(jnp.int32, (GATE_LANES, LANES, 2 * LANES), 1)
    col = lax.broadcasted_iota(jnp.int32, (GATE_LANES, LANES, 2 * LANES), 2)
    lane = lax.broadcasted_iota(jnp.int32, (GATE_LANES, LANES, 2 * LANES), 0)
    return (row == jnp.where(col < LANES, lane, lane + GATE_LANES)).astype(BF16)


def _mlstm_ctx(qkv, gates, bi, bf, sel, dh):
    bsz, t, w3 = qkv.shape
    dm = w3 // 3
    nc = t // CHUNK
    chunk = lambda p, s: s + p * (nc - 1 - 2 * s)
    kern = functools.partial(_mlstm_kernel, nc=nc, dh=dh, dm=dm, emit_h=False)
    st = lambda shape: pl.BlockSpec((1, 1) + shape, lambda b, p, s: (b, p) + (0,) * len(shape))
    shapes = _mlstm_state_shapes(dh)
    return pl.pallas_call(
        kern,
        grid=(bsz, 2, nc),
        in_specs=[pl.BlockSpec((1, CHUNK, w3), lambda b, p, s: (b, chunk(p, s), 0)),
                  pl.BlockSpec((1, CHUNK, 2 * LANES), lambda b, p, s: (b, chunk(p, s), 0)),
                  pl.BlockSpec((1, LANES), lambda b, p, s: (0, 0)),
                  pl.BlockSpec((1, LANES), lambda b, p, s: (0, 0)),
                  pl.BlockSpec(sel.shape, lambda b, p, s: (0, 0, 0))],
        out_specs=[st(sh) for sh in shapes],
        out_shape=[jax.ShapeDtypeStruct((bsz, 2) + sh, F32) for sh in shapes],
        scratch_shapes=[pltpu.VMEM(sh, F32) for sh in shapes],
        compiler_params=_cparams(("arbitrary",) * 3),
        name="mlstm_ctx",
    )(qkv, gates, bi, bf, sel)


def _mlstm_lat(qkv, gates, wide, bi, bf, sel, mh_g, c0, m0, dh):
    bsz, s, w3 = qkv.shape
    dm = w3 // 3
    nc = s // CHUNK
    chunk = lambda p, t: t + p * (nc - 1 - 2 * t)
    ochunk = lambda p, t: nc - 1 - p * t
    kern = functools.partial(_mlstm_kernel, nc=nc, dh=dh, dm=dm, emit_h=True)
    st = lambda shape: pl.BlockSpec((1, 1) + shape, lambda b, p, t: (b, p) + (0,) * len(shape))
    shapes = _mlstm_state_shapes(dh)
    return pl.pallas_call(
        kern,
        grid=(bsz, 2, nc),
        in_specs=[pl.BlockSpec((1, CHUNK, w3), lambda b, p, t: (b, chunk(p, t), 0)),
                  pl.BlockSpec((1, CHUNK, 2 * LANES), lambda b, p, t: (b, chunk(p, t), 0)),
                  pl.BlockSpec((1, CHUNK, dm), lambda b, p, t: (b, ochunk(p, t), 0)),
                  pl.BlockSpec((1, LANES), lambda b, p, t: (0, 0)),
                  pl.BlockSpec((1, LANES), lambda b, p, t: (0, 0)),
                  pl.BlockSpec(sel.shape, lambda b, p, t: (0, 0, 0)),
                  pl.BlockSpec((1, dm), lambda b, p, t: (0, 0))]
        + [st(sh) for sh in shapes],
        out_specs=pl.BlockSpec((1, CHUNK, dm), lambda b, p, t: (b, ochunk(p, t), 0)),
        out_shape=jax.ShapeDtypeStruct((bsz, s, dm), BF16),
        scratch_shapes=[pltpu.VMEM(sh, F32) for sh in shapes]
        + [pltpu.VMEM((nc, CHUNK, dm), F32)],
        compiler_params=_cparams(("arbitrary",) * 3),
        name="mlstm_lat",
    )(qkv, gates, wide, bi, bf, sel, mh_g.reshape(1, dm), c0, m0)


_LRU_TILE = 512
_CONV_TAPS_BEFORE = 2
_F32_TINY = float(jnp.finfo(jnp.float32).tiny)


def _gelu_tanh(x):
    return x * (0.5 * (1.0 + jnp.tanh(0.7978845608028654 * (x + 0.044715 * (x * x * x)))))


def _block_scan(a, u, carry, row8, rev):
    for d in (1, 2, 4):
        sh = SUBLANES - d if rev else d
        m = (row8 < SUBLANES - d) if rev else (row8 >= d)
        u = a * jnp.where(m, pltpu.roll(u, sh, 0), 0.0) + u
        a = a * jnp.where(m, pltpu.roll(a, sh, 0), 1.0)
    sh = SUBLANES - 1 if rev else 1
    m = (row8 < SUBLANES - 1) if rev else (row8 >= 1)
    h_after = u + a * carry
    h_before = jnp.where(m, pltpu.roll(h_after, sh, 0), carry)
    last = h_after[0:1, :] if rev else h_after[SUBLANES - 1:SUBLANES, :]
    return h_before, h_after, last


def _rglru_kernel(xr_ref, gr_ref, xctx_ref, cw_ref, cb_ref, w4_ref, b4_ref, lam_ref, y_ref,
                  xp_ref, af_ref, uf_ref, ab_ref, ub_ref, *, rows, cols, t_ctx):
    blk = LANES
    s = rows * cols
    cw = cw_ref[...]
    cb = cb_ref[...]
    w4 = w4_ref[0]
    b4 = b4_ref[0]
    sp = _softplus(-lam_ref[...])
    row8 = lax.broadcasted_iota(jnp.int32, (SUBLANES, blk), 0)
    before = _CONV_TAPS_BEFORE

    half_c_sp = (0.5 * LRU_C) * sp

    def gates_to_scratch(xc, r0, n):
        half_xc = 0.5 * xc

        def a_u(r_half, i_half, c_row):
            tr = jnp.tanh(r_half)
            neg_log_a = c_row * tr + c_row
            a = jnp.exp(-neg_log_a)
            one_minus_a2 = jnp.tanh(neg_log_a) * (a * a + 1.0)
            root = one_minus_a2 * lax.rsqrt(jnp.maximum(one_minus_a2, _F32_TINY))
            ti = jnp.tanh(i_half)
            return a, root * (half_xc * ti + half_xc)
        z = _dot(xc.astype(BF16), w4) + b4
        a, u = a_u(z[:, 0:blk], z[:, blk:2 * blk], half_c_sp[0:1])
        af_ref[pl.ds(r0, n), :] = a
        uf_ref[pl.ds(r0, n), :] = u
        a, u = a_u(z[:, 2 * blk:3 * blk], z[:, 3 * blk:4 * blk], half_c_sp[1:2])
        ab_ref[pl.ds(r0, n), :] = a
        ub_ref[pl.ds(r0, n), :] = u

    pad = SUBLANES
    zeros = jnp.zeros((pad, blk), F32)
    xp_ref[0:pad, :] = zeros
    xp_ref[pad:pad + t_ctx, :] = xctx_ref[0]
    xp_ref[pad + t_ctx:2 * pad + t_ctx, :] = zeros
    ext = xp_ref[0:t_ctx + 2 * pad, :]
    xc = cb
    for j in range(cw.shape[0]):
        xc = xc + cw[j:j + 1] * ext[pad - before + j:pad - before + j + t_ctx]
    gates_to_scratch(xc, 0, t_ctx)
    nb = t_ctx // SUBLANES

    def ctx_body(i, carry):
        cf, cbw = carry
        rf = pl.multiple_of(i * SUBLANES, SUBLANES)
        rb = pl.multiple_of((nb - 1 - i) * SUBLANES, SUBLANES)
        _, _, cf = _block_scan(af_ref[pl.ds(rf, SUBLANES), :], uf_ref[pl.ds(rf, SUBLANES), :],
                               cf, row8, False)
        _, _, cbw = _block_scan(ab_ref[pl.ds(rb, SUBLANES), :], ub_ref[pl.ds(rb, SUBLANES), :],
                                cbw, row8, True)
        return cf, cbw
    zero = jnp.zeros((1, blk), F32)
    h0f, h0b = lax.fori_loop(0, nb, ctx_body, (zero, zero), unroll=4)

    rowv = lax.broadcasted_iota(jnp.int32, (rows, blk), 0)

    def copy(i, c):
        r0 = pl.multiple_of(i * _LRU_TILE, _LRU_TILE)
        xp_ref[pl.ds(before * rows + r0, _LRU_TILE), :] = xr_ref[0, pl.ds(r0, _LRU_TILE), :]
        return c
    lax.fori_loop(0, s // _LRU_TILE, copy, 0)
    for k in range(before):
        src = (cols - before + k) * rows
        ext = xr_ref[0, src - SUBLANES:src + rows, :]
        xp_ref[k * rows:(k + 1) * rows, :] = jnp.where(rowv == 0, 0.0,
                                                       ext[SUBLANES - 1:SUBLANES - 1 + rows])
    ext = xr_ref[0, 0:rows + SUBLANES, :]
    xp_ref[(cols + before) * rows:(cols + before + 1) * rows, :] = jnp.where(
        rowv == rows - 1, 0.0, ext[1:1 + rows])

    def fill(i, c):
        r0 = pl.multiple_of(i * _LRU_TILE, _LRU_TILE)
        xc = cb
        for j in range(cw.shape[0]):
            xc = xc + cw[j:j + 1] * xp_ref[pl.ds(r0 + j * rows, _LRU_TILE), :]
        gates_to_scratch(xc, r0, _LRU_TILE)
        return c
    lax.fori_loop(0, s // _LRU_TILE, fill, 0)

    def pass1(i, carry):
        hf, pf, hb, pb = carry
        rf = pl.multiple_of(i * rows, rows)
        rb = pl.multiple_of((cols - 1 - i) * rows, rows)
        a = af_ref[pl.ds(rf, rows), :]
        hf = a * hf + uf_ref[pl.ds(rf, rows), :]
        pf = pf * a
        uf_ref[pl.ds(rf, rows), :] = hf
        af_ref[pl.ds(rf, rows), :] = pf
        a = ab_ref[pl.ds(rb, rows), :]
        hb = a * hb + ub_ref[pl.ds(rb, rows), :]
        pb = pb * a
        ub_ref[pl.ds(rb, rows), :] = hb
        ab_ref[pl.ds(rb, rows), :] = pb
        return hf, pf, hb, pb
    z64 = jnp.zeros((rows, blk), F32)
    o64 = jnp.ones((rows, blk), F32)
    hf_end, pf_end, hb_end, pb_end = lax.fori_loop(0, cols, pass1, (z64, o64, z64, o64), unroll=2)

    nrb = rows // SUBLANES
    carry = h0f
    hin_f = []
    for k in range(nrb):
        sl = slice(k * SUBLANES, (k + 1) * SUBLANES)
        before_k, _, carry = _block_scan(pf_end[sl], hf_end[sl], carry, row8, False)
        hin_f.append(before_k)
    carry = h0b
    hin_b = [None] * nrb
    for k in reversed(range(nrb)):
        sl = slice(k * SUBLANES, (k + 1) * SUBLANES)
        before_k, _, carry = _block_scan(pb_end[sl], hb_end[sl], carry, row8, True)
        hin_b[k] = before_k
    hin_f = jnp.concatenate(hin_f, axis=0)
    hin_b = jnp.concatenate(hin_b, axis=0)

    def emit(i, c):
        r0 = pl.multiple_of(i * rows, rows)
        hsum = (uf_ref[pl.ds(r0, rows), :] + af_ref[pl.ds(r0, rows), :] * hin_f
                + ub_ref[pl.ds(r0, rows), :] + ab_ref[pl.ds(r0, rows), :] * hin_b)
        y_ref[0, pl.ds(r0, rows), :] = (_gelu_tanh(gr_ref[0, pl.ds(r0, rows), :]) * hsum).astype(BF16)
        return c
    lax.fori_loop(0, cols, emit, 0, unroll=2)


def _rglru(wide, wide_ctx, conv_w, conv_b, w4, b4, lam, dm, rows, cols):
    bsz, s, _ = wide.shape
    t_ctx = wide_ctx.shape[1]
    nblk = dm // LANES
    assert conv_w.shape[0] == _CONV_TAPS_BEFORE + 2 and s == rows * cols
    kern = functools.partial(_rglru_kernel, rows=rows, cols=cols, t_ctx=t_ctx)
    col = lambda shape, off: pl.BlockSpec(shape, lambda b, n: (b, 0, off + n))
    par = lambda shape: pl.BlockSpec(shape, lambda b, n: (0,) * (len(shape) - 1) + (n,))
    return pl.pallas_call(
        kern,
        grid=(bsz, nblk),
        in_specs=[col((1, s, LANES), nblk), col((1, s, LANES), 2 * nblk),
                  col((1, t_ctx, LANES), nblk),
                  par((conv_w.shape[0], LANES)), par((1, LANES)),
                  pl.BlockSpec((1, LANES, 4 * LANES), lambda b, n: (n, 0, 0)),
                  pl.BlockSpec((1, 1, 4 * LANES), lambda b, n: (n, 0, 0)),
                  par((2, LANES))],
        out_specs=pl.BlockSpec((1, s, LANES), lambda b, n: (b, 0, n)),
        out_shape=jax.ShapeDtypeStruct((bsz, s, dm), BF16),
        scratch_shapes=[pltpu.VMEM((s + (conv_w.shape[0] - 1) * rows, LANES), F32)]
        + [pltpu.VMEM((s, LANES), F32)] * 4,
        compiler_params=_cparams(("arbitrary", "arbitrary")),
        name="rglru",
    )(wide, wide, wide_ctx, conv_w, conv_b.reshape(1, dm), w4, b4, lam)


def _outproj_kernel(ym_ref, yr_ref, w_ref, x_ref, gpost_ref, gpre_ref,
                    gate_ref, scale_ref, shift_ref, x1_ref, h2_ref, y_scr, *, dm, n_row_tiles):
    i = pl.program_id(0)
    both = jnp.logical_and

    def project(slot):
        y_scr[slot] = (_dot(ym_ref[...], w_ref[0:dm, :]) + _dot(yr_ref[...], w_ref[dm:2 * dm, :]))

    def finish(slot):
        x1 = x_ref[...] + gate_ref[0] * _rms(y_scr[slot], gpost_ref[...])
        x1_ref[...] = x1
        h2 = _rms(x1, gpre_ref[...]) * (1.0 + scale_ref[0]) + shift_ref[0]
        h2_ref[...] = h2.astype(BF16)

    for slot in (0, 1):
        mine = (i % 2) == slot

        @pl.when(both(mine, i == 0))
        def _(slot=slot):
            project(slot)

        @pl.when(both(mine, both(i > 0, i < n_row_tiles)))
        def _(slot=slot):
            finish(1 - slot)
            project(slot)

        @pl.when(both(mine, i == n_row_tiles))
        def _(slot=slot):
            finish(1 - slot)


def _outproj(ym, yr, w_out, x_cm, g_post, g_pre, mod3, tm, tps):
    n, d = x_cm.shape
    dm = ym.shape[1]
    nt = n // tm
    kern = functools.partial(_outproj_kernel, dm=dm, n_row_tiles=nt)
    cur = lambda i: jnp.minimum(i, nt - 1)
    prev = lambda i: jnp.maximum(i - 1, 0)
    modspec = lambda k: pl.BlockSpec((1, 1, d), lambda i: (prev(i) // tps, 0, k))
    vec = pl.BlockSpec((1, d), lambda i: (0, 0))
    return pl.pallas_call(
        kern,
        grid=(nt + 1,),
        in_specs=[pl.BlockSpec((tm, dm), lambda i: (cur(i), 0)),
                  pl.BlockSpec((tm, dm), lambda i: (cur(i), 0)),
                  pl.BlockSpec((2 * dm, d), lambda i: (0, 0), pipeline_mode=pl.Buffered(1)),
                  pl.BlockSpec((tm, d), lambda i: (prev(i), 0)),
                  vec, vec, modspec(2), modspec(4), modspec(3)],
        out_specs=[pl.BlockSpec((tm, d), lambda i: (prev(i), 0)),
                   pl.BlockSpec((tm, d), lambda i: (prev(i), 0))],
        out_shape=[jax.ShapeDtypeStruct((n, d), F32), jax.ShapeDtypeStruct((n, d), BF16)],
        scratch_shapes=[pltpu.VMEM((2, tm, d), F32)],
        compiler_params=_cparams(("arbitrary",)),
        name="outproj",
    )(ym, yr, w_out, x_cm, g_post.reshape(1, d), g_pre.reshape(1, d), mod3, mod3, mod3)


def _ffn_in_kernel(h_ref, wg_ref, wu_ref, o_ref):
    h = h_ref[...]
    g = _dot(h, wg_ref[...].astype(BF16))
    u = _dot(h, wu_ref[...].astype(BF16))
    o_ref[...] = ((g * _sigmoid(g)) * u).astype(BF16)


def _ffn_in(h2, w_in, d_ff, tm, tn):
    n, d = h2.shape
    nj = d_ff // tn
    return pl.pallas_call(
        _ffn_in_kernel,
        grid=(n // tm, nj),
        in_specs=[pl.BlockSpec((tm, d), lambda i, j: (i, 0)),
                  pl.BlockSpec((d, tn), lambda i, j: (0, j)),
                  pl.BlockSpec((d, tn), lambda i, j: (0, j + nj))],
        out_specs=pl.BlockSpec((tm, tn), lambda i, j: (i, j)),
        out_shape=jax.ShapeDtypeStruct((n, d_ff), BF16),
        compiler_params=_cparams(("arbitrary", "arbitrary")),
        name="ffn_in",
    )(h2, w_in, w_in)


def _ffn_out_kernel(a_ref, w_ref, x1_ref, g_ref, gate_ref, o_hbm, acc_ref, fin_ref, res_ref, sem,
                    *, n_row_tiles, tiles_per_sample):
    i = pl.program_id(0)
    k = pl.program_id(1)
    both = jnp.logical_and
    scatter = lambda tile: _grid_tile_copies(o_hbm, res_ref, sem, tile, tiles_per_sample, True)

    def finish():
        res_ref[...] = x1_ref[...] + gate_ref[0] * _rms(fin_ref[(i + 1) % 2], g_ref[...])
        for cp in scatter(i - 1):
            cp.start(priority=GRID_DMA_PRIORITY)

    def partial_dot():
        return _dot(a_ref[...], w_ref[...])

    @pl.when(both(k == 0, i >= 2))
    def _():
        for cp in scatter(i - 2):
            cp.wait()

    @pl.when(both(k == 0, i == 0))
    def _():
        acc_ref[...] = partial_dot()

    @pl.when(both(k == 0, both(i > 0, i < n_row_tiles)))
    def _():
        acc_ref[...] = partial_dot()
        finish()

    @pl.when(both(k == 0, i == n_row_tiles))
    def _():
        finish()

    @pl.when(both(k == 1, i < n_row_tiles))
    def _():
        fin_ref[i % 2] = acc_ref[...] + partial_dot()

    @pl.when(both(k == 1, i == n_row_tiles))
    def _():
        for cp in scatter(i - 1):
            cp.wait()


def _ffn_out(act, w_out, x1, g_post, mod3, bsz, rows, cols):
    n, d_ff = act.shape
    d = x1.shape[1]
    tm = rows * COLS_PER_TILE
    tps = cols // COLS_PER_TILE
    nt = n // tm
    n_k = 2
    tk = d_ff // n_k
    assert nt >= 2 and tk % LANES == 0
    kern = functools.partial(_ffn_out_kernel, n_row_tiles=nt, tiles_per_sample=tps)
    cur = lambda i: jnp.minimum(i, nt - 1)
    prev = lambda i: jnp.maximum(i - 1, 0)
    kk = lambda i, k: jnp.where(i < nt, k, n_k - 1)
    return pl.pallas_call(
        kern,
        grid=(nt + 1, n_k),
        in_specs=[pl.BlockSpec((tm, tk), lambda i, k: (cur(i), kk(i, k))),
                  pl.BlockSpec((tk, d), lambda i, k: (kk(i, k), 0)),
                  pl.BlockSpec((tm, d), lambda i, k: (prev(i), 0)),
                  pl.BlockSpec((1, d), lambda i, k: (0, 0)),
                  pl.BlockSpec((1, 1, d), lambda i, k: (prev(i) // tps, 0, 5))],
        out_specs=pl.BlockSpec(memory_space=pl.ANY),
        out_shape=jax.ShapeDtypeStruct((bsz, rows, cols, d), F32),
        scratch_shapes=[pltpu.VMEM((tm, d), F32), pltpu.VMEM((2, tm, d), F32),
                        pltpu.VMEM((tm, d), F32), pltpu.SemaphoreType.DMA((COLS_PER_TILE,))],
        compiler_params=_cparams(("arbitrary", "arbitrary")),
        name="ffn_out",
    )(act, w_out, x1, g_post.reshape(1, d), mod3)


def kernel(x, c, ctx, c_ctx, w_mod, b_mod, g_pre_mix, g_post_mix, g_pre_ffn, g_post_ffn,
           w_in, b_gates, mh_norm_g, conv_w, conv_b, lru_w_a, lru_b_a, lru_w_x, lru_b_x,
           lru_lambda, w_out, w_ffn_in, w_ffn_out):
    bsz, s, d = x.shape
    t_ctx = ctx.shape[1]
    depth = w_mod.shape[0]
    assert depth == 1, "context-stream update between layers is not implemented"
    cols = GRID_W
    rows = s // cols
    dm = mh_norm_g.shape[1]
    assert dm == conv_w.shape[2]
    dh = dm // N_HEADS
    n_gates = N_GATE_TYPES * N_HEADS
    d_ff = w_ffn_out.shape[1]
    layer = 0

    wt = jnp.swapaxes(w_in[layer], 0, 1)
    o4 = 4 * dm
    o5 = o4 + n_gates
    wt_all = wt.astype(BF16)
    wgt = wt[o4:o5].reshape(N_GATE_TYPES, N_HEADS, d)
    w_ig = jnp.concatenate([wgt[0], wgt[2]], axis=0)
    w_fg = jnp.concatenate([wgt[1], wgt[3]], axis=0)
    zpad = lambda n: jnp.zeros((n, d), wt.dtype)
    wt_gate = jnp.concatenate([zpad(GATE_LANES), w_ig, zpad(LANES - 2 * GATE_LANES),
                               w_fg, w_fg, zpad(LANES - 2 * GATE_LANES)], axis=0).astype(BF16)
    bg = b_gates[layer].astype(F32)
    b_ig = jnp.concatenate([bg[0], bg[2]])
    b_fg = jnp.concatenate([bg[1], bg[3]])
    bpad = lambda n: jnp.zeros((n,), F32)
    bi = jnp.concatenate([bpad(GATE_LANES), b_ig, bpad(LANES - 2 * GATE_LANES)]).reshape(1, LANES)
    bf = jnp.concatenate([b_fg, b_fg, bpad(LANES - 2 * GATE_LANES)]).reshape(1, LANES)
    sel = _mlstm_select()
    nblk = lru_w_a.shape[2]
    wa, wx = lru_w_a[layer], lru_w_x[layer]
    w4 = (0.5 * jnp.concatenate([wa[0], wx[0], wa[1], wx[1]], axis=2)).astype(BF16)
    ba = lru_b_a[layer].reshape(2, nblk, 1, LANES)
    bx = lru_b_x[layer].reshape(2, nblk, 1, LANES)
    b4 = 0.5 * jnp.concatenate([ba[0], bx[0], ba[1], bx[1]], axis=2)
    w_o = w_out[layer].astype(BF16)
    w_f1 = w_ffn_in[layer]
    w_f2 = w_ffn_out[layer].astype(BF16)

    n_rows = SUBLANES * ((bsz + 1 + SUBLANES - 1) // SUBLANES)
    c_rows = jnp.concatenate([c, c_ctx[None], jnp.zeros((n_rows - bsz - 1, d), c.dtype)], axis=0)
    mod3 = _modulation(c_rows, w_mod[layer], b_mod[layer]).reshape(n_rows, 1, 6 * d)

    x4 = x.reshape(bsz, rows, cols, d)
    tm = rows * COLS_PER_TILE
    tps = cols // COLS_PER_TILE
    qkv, wide, gates, x_cm = _inproj(x4, g_pre_mix[layer], mod3, lambda i: i // tps,
                                     wt_all, wt_gate, 3 * dm, (o4, n_gates), tm, 1536)
    qkv_c, wide_c, gates_c = _inproj(ctx.reshape(bsz * t_ctx, d), g_pre_mix[layer], mod3,
                                     lambda i: bsz, wt_all, wt_gate, 3 * dm, (o4, n_gates), tm, 1536)
    wide = wide.reshape(bsz, s, 3 * dm)
    wide_c = wide_c.reshape(bsz, t_ctx, 3 * dm)

    c0, m0 = _mlstm_ctx(qkv_c.reshape(bsz, t_ctx, 3 * dm),
                        gates_c.reshape(bsz, t_ctx, 2 * LANES), bi, bf, sel, dh)
    y_m = _mlstm_lat(qkv.reshape(bsz, s, 3 * dm), gates.reshape(bsz, s, 2 * LANES), wide,
                     bi, bf, sel, mh_norm_g[layer], c0, m0, dh)

    y_r = _rglru(wide, wide_c, conv_w[layer], conv_b[layer], w4, b4, lru_lambda[layer],
                 dm, rows, cols)

    x1, h2 = _outproj(y_m.reshape(bsz * s, dm), y_r.reshape(bsz * s, dm), w_o, x_cm,
                      g_post_mix[layer], g_pre_ffn[layer], mod3, tm, tps)
    act = _ffn_in(h2, w_f1, d_ff, 1024, 512)
    out = _ffn_out(act, w_f2, x1, g_post_ffn[layer], mod3, bsz, rows, cols)
    return out.reshape(bsz, s, d)
```
